```python
import math
import jax, jax.numpy as jnp
from jax import lax
import numpy as np

D_MODEL = 4096
BATCH = 4
SEQ = 4096
DEPTH = 1

HEAD_DIM = 128
N_Q_HEADS = 16
N_KV_HEADS = 4
GQA_GROUP = N_Q_HEADS // N_KV_HEADS
ATTN_W = N_Q_HEADS * HEAD_DIM
KV_W = N_KV_HEADS * HEAD_DIM
WINDOW = 128
BLOCK = 128
N_BUCKETS = 32
MAX_DISTANCE = 128
GMLP_HEADS = 16
GMLP_HEAD_DIM = 128
GMLP_W = GMLP_HEADS * GMLP_HEAD_DIM
CHUNK = 128
MIX_W = ATTN_W + GMLP_W
IN_W = ATTN_W + 2 * KV_W + 2 * GMLP_W
D_FF = 4 * D_MODEL
EPS = 1e-6
NEG = -1e30

kernel_name = "hybrid_window_gqa_gmlp_block"


def rmsnorm(x, g):
    xf = x.astype(jnp.float32)
    y = xf * lax.rsqrt(jnp.mean(xf * xf, axis=-1, keepdims=True) + EPS)
    return (y * g.astype(jnp.float32)).astype(x.dtype)


def t5_bucket(rel):
    nb = N_BUCKETS // 2
    max_exact = nb // 2
    ret = (rel > 0).astype(np.int32) * nb
    n = np.abs(rel)
    large = max_exact + (np.log(np.maximum(n, 1).astype(np.float32) / max_exact)
                         / math.log(MAX_DISTANCE / max_exact) * (nb - max_exact)).astype(np.int32)
    large = np.minimum(large, nb - 1)
    return ret + np.where(n < max_exact, n, large)


def band_layout(seq):
    nblk = seq // BLOCK
    a = np.arange(BLOCK)[:, None]
    s = np.arange(3 * BLOCK)[None, :]
    rel = s - BLOCK - a
    blk = np.arange(nblk)[:, None, None]
    k_pos = (blk - 1) * BLOCK + s[None]
    valid = (k_pos >= 0) & (k_pos < seq) & (np.abs(rel)[None] <= WINDOW)
    return rel, valid


def banded(t, nblk):
    B, _, H, D = t.shape
    tp = jnp.pad(t, ((0, 0), (BLOCK, BLOCK), (0, 0), (0, 0)))
    tb = tp.reshape(B, nblk + 2, BLOCK, H, D)
    return jnp.concatenate([tb[:, :-2], tb[:, 1:-1], tb[:, 2:]], axis=2)


def windowed_gqa(q, k, v, q_gain, k_gain, rel_bias, sink):
    B, S, _ = q.shape
    nblk = S // BLOCK
    q = rmsnorm(q.reshape(B, S, N_Q_HEADS, HEAD_DIM), q_gain)
    k = rmsnorm(k.reshape(B, S, N_KV_HEADS, HEAD_DIM), k_gain)
    v = v.reshape(B, S, N_KV_HEADS, HEAD_DIM)
    qb = q.reshape(B, nblk, BLOCK, N_KV_HEADS, GQA_GROUP, HEAD_DIM)
    kb = banded(k, nblk)
    vb = banded(v, nblk)
    rel, valid = band_layout(S)
    bias = rel_bias[jnp.asarray(t5_bucket(rel))]
    bias = jnp.transpose(bias, (2, 0, 1)).reshape(N_KV_HEADS, GQA_GROUP, BLOCK, 3 * BLOCK)
    s = jnp.einsum('bnqkgd,bnskd->bnkgqs', qb, kb).astype(jnp.float32) * (HEAD_DIM ** -0.5)
    s = jnp.where(jnp.asarray(valid)[None, :, None, None], s + bias.astype(jnp.float32), NEG)
    sink_l = sink.astype(jnp.float32).reshape(N_KV_HEADS, GQA_GROUP)[:, :, None, None]
    m = jnp.maximum(jnp.max(s, axis=-1, keepdims=True), sink_l)
    p = jnp.exp(s - m)
    p = p / (jnp.sum(p, axis=-1, keepdims=True) + jnp.exp(sink_l - m))
    o = jnp.einsum('bnkgqs,bnskd->bnqkgd', p.astype(v.dtype), vb)
    return o.reshape(B, S, ATTN_W)


def spatial_gating(u, v, v_gain, w_s, b_s):
    B, S, _ = u.shape
    nchunk = S // CHUNK
    u = jax.nn.gelu(u)
    v = rmsnorm(jax.nn.gelu(v), v_gain)
    vc = v.reshape(B, nchunk, CHUNK, GMLP_HEADS, GMLP_HEAD_DIM)
    sv = jnp.einsum('hts,bcshd->bcthd', w_s, vc) + b_s.T[None, None, :, :, None]
    return u * sv.reshape(B, S, GMLP_W)


def setup_inputs(seed: int = 0) -> dict:
    key = jax.random.key(seed)
    ks = jax.random.split(key, 16)
    f = jnp.float32
    L = DEPTH
    nrm = lambda k, shape, sc: jax.random.normal(k, shape, f) * sc
    return {
        "x": nrm(ks[0], (BATCH, SEQ, D_MODEL), 1.0),
        "norm1": 1.0 + nrm(ks[1], (L, D_MODEL), 0.02),
        "w_in": nrm(ks[2], (L, D_MODEL, IN_W), D_MODEL ** -0.5),
        "q_gain": 1.0 + nrm(ks[3], (L, HEAD_DIM), 0.02),
        "k_gain": 1.0 + nrm(ks[4], (L, HEAD_DIM), 0.02),
        "rel_bias": nrm(ks[5], (N_BUCKETS, N_Q_HEADS), 0.5),
        "attn_sink": nrm(ks[6], (L, N_Q_HEADS), 0.5),
        "attn_out_gain": 1.0 + nrm(ks[7], (L, ATTN_W), 0.02),
        "gmlp_v_gain": 1.0 + nrm(ks[8], (L, GMLP_W), 0.02),
        "gmlp_w_s": nrm(ks[9], (L, GMLP_HEADS, CHUNK, CHUNK), CHUNK ** -0.5),
        "gmlp_b_s": 1.0 + nrm(ks[10], (L, GMLP_HEADS, CHUNK), 0.1),
        "gmlp_out_gain": 1.0 + nrm(ks[11], (L, GMLP_W), 0.02),
        "w_out": nrm(ks[12], (L, MIX_W, D_MODEL), MIX_W ** -0.5),
        "norm2": 1.0 + nrm(ks[13], (L, D_MODEL), 0.02),
        "w1": nrm(ks[14], (L, D_MODEL, D_FF), D_MODEL ** -0.5),
        "w2": nrm(ks[15], (L, D_FF, D_MODEL), D_FF ** -0.5),
    }


def reference(x, norm1, w_in, q_gain, k_gain, rel_bias, attn_sink, attn_out_gain,
              gmlp_v_gain, gmlp_w_s, gmlp_b_s, gmlp_out_gain, w_out, norm2, w1, w2):
    o_k = ATTN_W
    o_v = o_k + KV_W
    o_u = o_v + KV_W
    o_g = o_u + GMLP_W
    for l in range(DEPTH):
        h = rmsnorm(x, norm1[l])
        z = jnp.einsum('bsd,de->bse', h, w_in[l])
        a = windowed_gqa(z[..., :o_k], z[..., o_k:o_v], z[..., o_v:o_u],
                         q_gain[l], k_gain[l], rel_bias, attn_sink[l])
        g = spatial_gating(z[..., o_u:o_g], z[..., o_g:], gmlp_v_gain[l],
                           gmlp_w_s[l], gmlp_b_s[l])
        mix = jnp.concatenate([rmsnorm(a, attn_out_gain[l]), rmsnorm(g, gmlp_out_gain[l])], axis=-1)
        x = x + jnp.einsum('bse,ed->bsd', mix, w_out[l])
        h = rmsnorm(x, norm2[l])
        hid = jnp.square(jax.nn.relu(jnp.einsum('bsd,df->bsf', h, w1[l])))
        x = x + jnp.einsum('bsf,fd->bsd', hid, w2[l])
    return x
```

```python
import functools
import math

import jax
import jax.numpy as jnp
import numpy as np
from jax import lax
from jax.experimental import pallas as pl
from jax.experimental.pallas import tpu as pltpu

HEAD_DIM = 128
N_Q_HEADS = 16
N_KV_HEADS = 4
GQA_GROUP = N_Q_HEADS // N_KV_HEADS
ATTN_W = N_Q_HEADS * HEAD_DIM
KV_W = N_KV_HEADS * HEAD_DIM
WINDOW = 128
BLOCK = 128
N_BUCKETS = 32
MAX_DISTANCE = 128
GMLP_HEADS = 16
GMLP_W = GMLP_HEADS * HEAD_DIM
EPS = 1e-6
NEG = -1e30

F32 = jnp.float32
BF16 = jnp.bfloat16

V7X_VMEM_LIMIT_BYTES = 60 * 1024 * 1024


def _params(sem):
    return pltpu.CompilerParams(dimension_semantics=sem,
                                vmem_limit_bytes=V7X_VMEM_LIMIT_BYTES)


def _prep_kernel(x_ref, g_ref, xg_ref, inv_ref):
    x = x_ref[...]
    d = x.shape[-1]
    ss = jnp.sum(x * x, axis=-1, keepdims=True)
    inv_ref[...] = lax.rsqrt(ss * (1.0 / d) + EPS)
    xg_ref[...] = (x * g_ref[...]).astype(BF16)


def _prep(x2d, gain, rows=256):
    m, d = x2d.shape
    return pl.pallas_call(
        _prep_kernel,
        grid=(m // rows,),
        in_specs=[pl.BlockSpec((rows, d), lambda i: (i, 0)),
                  pl.BlockSpec((1, d), lambda i: (0, 0))],
        out_specs=[pl.BlockSpec((rows, d), lambda i: (i, 0)),
                   pl.BlockSpec((rows, 1), lambda i: (i, 0))],
        out_shape=[jax.ShapeDtypeStruct((m, d), BF16),
                   jax.ShapeDtypeStruct((m, 1), F32)],
        compiler_params=_params(("arbitrary",)),
        name="prep_norm",
    )(x2d, gain.reshape(1, d))


def _inproj_kernel(a_ref, b_ref, inv_ref, o_ref):
    acc = jnp.dot(a_ref[...], b_ref[...], preferred_element_type=F32)
    o_ref[...] = (acc * inv_ref[...]).astype(o_ref.dtype)


def _inproj(a, b, inv, tm=1024, tn=1024):
    m, k = a.shape
    n = b.shape[1]
    return pl.pallas_call(
        _inproj_kernel,
        grid=(m // tm, n // tn),
        in_specs=[pl.BlockSpec((tm, k), lambda i, j: (i, 0)),
                  pl.BlockSpec((k, tn), lambda i, j: (0, j)),
                  pl.BlockSpec((tm, 1), lambda i, j: (i, 0))],
        out_specs=pl.BlockSpec((tm, tn), lambda i, j: (i, j)),
        out_shape=jax.ShapeDtypeStruct((m, n), BF16),
        compiler_params=_params(("arbitrary", "arbitrary")),
        name="in_proj",
    )(a, b, inv)


def _up_kernel(a_ref, b_ref, inv_ref, o_ref):
    acc = jnp.dot(a_ref[...], b_ref[...], preferred_element_type=F32)
    r = jnp.maximum(acc * inv_ref[...], 0.0)
    o_ref[...] = (r * r).astype(o_ref.dtype)


def _up(a, b, inv, tm=1024, tn=1024):
    m, k = a.shape
    n = b.shape[1]
    return pl.pallas_call(
        _up_kernel,
        grid=(m // tm, n // tn),
        in_specs=[pl.BlockSpec((tm, k), lambda i, j: (i, 0)),
                  pl.BlockSpec((k, tn), lambda i, j: (0, j)),
                  pl.BlockSpec((tm, 1), lambda i, j: (i, 0))],
        out_specs=pl.BlockSpec((tm, tn), lambda i, j: (i, j)),
        out_shape=jax.ShapeDtypeStruct((m, n), BF16),
        compiler_params=_params(("arbitrary", "arbitrary")),
        name="mlp_up",
    )(a, b, inv)


def _outproj_kernel(a_ref, b_ref, x_ref, g_ref, x1_ref, x1g_ref, inv_ref, ss_ref, *, d_model):
    j = pl.program_id(1)
    acc = jnp.dot(a_ref[...], b_ref[...], preferred_element_type=F32)
    x1 = x_ref[...] + acc
    x1_ref[...] = x1
    x1g_ref[...] = (x1 * g_ref[...]).astype(BF16)
    part = jnp.sum(x1 * x1, axis=-1, keepdims=True)

    @pl.when(j == 0)
    def _():
        ss_ref[...] = part

    @pl.when(j > 0)
    def _():
        ss_ref[...] += part

    @pl.when(j == pl.num_programs(1) - 1)
    def _():
        inv_ref[...] = lax.rsqrt(ss_ref[...] * (1.0 / d_model) + EPS)


def _outproj(a, b, x2d, gain, tm=1024, tn=512):
    m, k = a.shape
    n = b.shape[1]
    return pl.pallas_call(
        functools.partial(_outproj_kernel, d_model=n),
        grid=(m // tm, n // tn),
        in_specs=[pl.BlockSpec((tm, k), lambda i, j: (i, 0)),
                  pl.BlockSpec((k, tn), lambda i, j: (0, j)),
                  pl.BlockSpec((tm, tn), lambda i, j: (i, j)),
                  pl.BlockSpec((1, tn), lambda i, j: (0, j))],
        out_specs=[pl.BlockSpec((tm, tn), lambda i, j: (i, j)),
                   pl.BlockSpec((tm, tn), lambda i, j: (i, j)),
                   pl.BlockSpec((tm, 1), lambda i, j: (i, 0))],
        out_shape=[jax.ShapeDtypeStruct((m, n), F32),
                   jax.ShapeDtypeStruct((m, n), BF16),
                   jax.ShapeDtypeStruct((m, 1), F32)],
        scratch_shapes=[pltpu.VMEM((tm, 1), F32)],
        compiler_params=_params(("arbitrary", "arbitrary")),
        name="out_proj",
    )(a, b, x2d, gain.reshape(1, n))


def _down_kernel(a_ref, b_ref, x1_ref, o_ref):
    @pl.when(pl.program_id(2) == 0)
    def _():
        o_ref[...] = x1_ref[...]

    o_ref[...] += jnp.dot(a_ref[...], b_ref[...], preferred_element_type=F32)


def _down(a, b, x1, tm=1024, tn=1024, tk=2048):
    m, k = a.shape
    n = b.shape[1]
    return pl.pallas_call(
        _down_kernel,
        grid=(m // tm, n // tn, k // tk),
        in_specs=[pl.BlockSpec((tm, tk), lambda i, j, kk: (i, kk)),
                  pl.BlockSpec((tk, tn), lambda i, j, kk: (kk, j)),
                  pl.BlockSpec((tm, tn), lambda i, j, kk: (i, j))],
        out_specs=pl.BlockSpec((tm, tn), lambda i, j, kk: (i, j)),
        out_shape=jax.ShapeDtypeStruct((m, n), F32),
        compiler_params=_params(("arbitrary", "arbitrary", "arbitrary")),
        name="mlp_down",
    )(a, b, x1)


def _t5_bucket(rel):
    nb = N_BUCKETS // 2
    max_exact = nb // 2
    ret = (rel > 0).astype(np.int32) * nb
    n = np.abs(rel)
    large = max_exact + (np.log(np.maximum(n, 1).astype(np.float32) / max_exact)
                         / math.log(MAX_DISTANCE / max_exact) * (nb - max_exact)).astype(np.int32)
    large = np.minimum(large, nb - 1)
    return ret + np.where(n < max_exact, n, large)


def _bucket_tables():
    a = np.arange(BLOCK)[:, None]
    s = np.arange(3 * BLOCK)[None, :]
    rel = s - BLOCK - a
    bucket = _t5_bucket(rel).astype(np.int32)
    in_window = np.abs(rel) <= WINDOW
    col = np.broadcast_to(s, rel.shape)
    first = in_window & (col >= BLOCK)
    middle = in_window
    last = in_window & (col < 2 * BLOCK)
    return np.stack([np.where(v, bucket, -1) for v in (first, middle, last)]).astype(np.int32)


def _bias_kernel(rb_ref, bk_ref, o_ref):
    h = pl.program_id(1)
    bk = bk_ref[...]
    acc = jnp.full(bk.shape, NEG, F32)
    for b in range(N_BUCKETS):
        acc = jnp.where(bk == b, rb_ref[b, h], acc)
    o_ref[...] = acc


def _bias_tables(rel_bias):
    buckets = jnp.asarray(_bucket_tables())
    return pl.pallas_call(
        _bias_kernel,
        grid=(3, N_Q_HEADS),
        in_specs=[pl.BlockSpec(memory_space=pltpu.SMEM),
                  pl.BlockSpec((None, BLOCK, 3 * BLOCK), lambda v, h: (v, 0, 0))],
        out_specs=pl.BlockSpec((None, None, BLOCK, 3 * BLOCK), lambda v, h: (v, h, 0, 0)),
        out_shape=jax.ShapeDtypeStruct((3, N_Q_HEADS, BLOCK, 3 * BLOCK), F32),
        compiler_params=_params(("arbitrary", "arbitrary")),
        name="bias_tables",
    )(rel_bias, buckets)


def _rms_scale(t):
    return lax.rsqrt(jnp.sum(t * t, axis=-1, keepdims=True) * (1.0 / t.shape[-1]) + EPS)


def _mix_kernel(sink_ref, q_ref, kp_ref, ko_ref, kn_ref, vp_ref, vo_ref, vn_ref,
                u0_ref, u1_ref, g0_ref, g1_ref, bias_ref, qg_ref, kg_ref, ag_ref,
                vg_ref, ws_ref, bt_ref, gg_ref, mix_ref, a_scr, g_scr):
    n = pl.program_id(1)
    nlast = pl.num_programs(1) - 1
    variant = jnp.where(n == 0, 0, jnp.where(n == nlast, 2, 1))
    scale = HEAD_DIM ** -0.5
    qgain = qg_ref[...]
    kgain = kg_ref[...]

    kband = jnp.concatenate([kp_ref[...], ko_ref[...], kn_ref[...]], axis=0).astype(F32)
    vband = jnp.concatenate([vp_ref[...], vo_ref[...], vn_ref[...]], axis=0)
    for kh in range(N_KV_HEADS):
        k = kband[:, kh * HEAD_DIM:(kh + 1) * HEAD_DIM]
        kn = (k * _rms_scale(k) * kgain).astype(BF16)
        v = vband[:, kh * HEAD_DIM:(kh + 1) * HEAD_DIM]
        qs = []
        for g in range(GQA_GROUP):
            h = kh * GQA_GROUP + g
            qh = q_ref[:, h * HEAD_DIM:(h + 1) * HEAD_DIM].astype(F32)
            qs.append((qh * _rms_scale(qh) * qgain).astype(BF16))
        qstack = jnp.concatenate(qs, axis=0)
        s = lax.dot_general(qstack, kn, (((1,), (1,)), ((), ())),
                            preferred_element_type=F32)
        ps, ls = [], []
        for g in range(GQA_GROUP):
            h = kh * GQA_GROUP + g
            sink = sink_ref[h]
            sg = s[g * BLOCK:(g + 1) * BLOCK] * scale + bias_ref[variant, h]
            m = jnp.maximum(jnp.max(sg, axis=-1, keepdims=True), sink)
            p = jnp.exp(sg - m)
            ls.append(jnp.sum(p, axis=-1, keepdims=True) + jnp.exp(sink - m))
            ps.append(p.astype(BF16))
        pstack = jnp.concatenate(ps, axis=0)
        o = jnp.dot(pstack, v, preferred_element_type=F32)
        for g in range(GQA_GROUP):
            h = kh * GQA_GROUP + g
            a_scr[:, h * HEAD_DIM:(h + 1) * HEAD_DIM] = o[g * BLOCK:(g + 1) * BLOCK] / ls[g]

    a = a_scr[...]
    mix_ref[:, :ATTN_W] = (a * _rms_scale(a) * ag_ref[...]).astype(mix_ref.dtype)

    u = jnp.concatenate([u0_ref[...], u1_ref[...]], axis=1).astype(F32)
    gv = jnp.concatenate([g0_ref[...], g1_ref[...]], axis=1).astype(F32)
    gu = jax.nn.gelu(u, approximate=True)
    gv = jax.nn.gelu(gv, approximate=True)
    vn_ = (gv * _rms_scale(gv) * vg_ref[...]).astype(BF16)
    for h in range(GMLP_HEADS):
        sl = slice(h * HEAD_DIM, (h + 1) * HEAD_DIM)
        sv = jnp.dot(ws_ref[h], vn_[:, sl], preferred_element_type=F32) + bt_ref[:, h:h + 1]
        g_scr[:, sl] = gu[:, sl] * sv
    gg = g_scr[...]
    mix_ref[:, ATTN_W:] = (gg * _rms_scale(gg) * gg_ref[...]).astype(mix_ref.dtype)


def _mix(z3, bias, sink, q_gain, k_gain, attn_out_gain, v_gain, w_s, b_t, gmlp_out_gain):
    bsz, seq, _ = z3.shape
    nblk = seq // BLOCK
    assert nblk >= 2
    kcol = ATTN_W // KV_W
    vcol = kcol + 1
    ucol = (ATTN_W + 2 * KV_W) // 1024
    gcol = ucol + 2

    def zspec(width, col, shift=0):
        def imap(b, n):
            return (b, jnp.clip(n + shift, 0, nblk - 1), col)
        return pl.BlockSpec((None, BLOCK, width), imap)

    def full(shape):
        return pl.BlockSpec(shape, lambda b, n: (0,) * len(shape))

    in_specs = [
        pl.BlockSpec(memory_space=pltpu.SMEM),
        zspec(ATTN_W, 0),
        zspec(KV_W, kcol, -1), zspec(KV_W, kcol), zspec(KV_W, kcol, 1),
        zspec(KV_W, vcol, -1), zspec(KV_W, vcol), zspec(KV_W, vcol, 1),
        zspec(1024, ucol), zspec(1024, ucol + 1),
        zspec(1024, gcol), zspec(1024, gcol + 1),
        pl.BlockSpec(bias.shape, lambda b, n: (0, 0, 0, 0), pipeline_mode=pl.Buffered(1)),
        full((1, HEAD_DIM)), full((1, HEAD_DIM)), full((1, ATTN_W)),
        full((1, GMLP_W)), full(w_s.shape), full(b_t.shape), full((1, GMLP_W)),
    ]
    return pl.pallas_call(
        _mix_kernel,
        grid=(bsz, nblk),
        in_specs=in_specs,
        out_specs=pl.BlockSpec((None, BLOCK, ATTN_W + GMLP_W), lambda b, n: (b, n, 0)),
        out_shape=jax.ShapeDtypeStruct((bsz, seq, ATTN_W + GMLP_W), BF16),
        scratch_shapes=[pltpu.VMEM((BLOCK, ATTN_W), F32), pltpu.VMEM((BLOCK, GMLP_W), F32)],
        compiler_params=_params(("arbitrary", "arbitrary")),
        name="mix",
    )(sink, z3, z3, z3, z3, z3, z3, z3, z3, z3, z3, z3, bias,
      q_gain.reshape(1, HEAD_DIM), k_gain.reshape(1, HEAD_DIM), attn_out_gain.reshape(1, ATTN_W),
      v_gain.reshape(1, GMLP_W), w_s, b_t, gmlp_out_gain.reshape(1, GMLP_W))


def kernel(x, norm1, w_in, q_gain, k_gain, rel_bias, attn_sink, attn_out_gain,
           gmlp_v_gain, gmlp_w_s, gmlp_b_s, gmlp_out_gain, w_out, norm2, w1, w2):
    bsz, seq, d = x.shape
    depth = norm1.shape[0]
    bias = _bias_tables(rel_bias)
    x2d = x.reshape(bsz * seq, d)
    for l in range(depth):
        xg, inv1 = _prep(x2d, norm1[l])
        z = _inproj(xg, w_in[l].astype(BF16), inv1)
        mix = _mix(z.reshape(bsz, seq, -1), bias, attn_sink[l], q_gain[l], k_gain[l],
                   attn_out_gain[l], gmlp_v_gain[l], gmlp_w_s[l].astype(BF16),
                   gmlp_b_s[l].T, gmlp_out_gain[l])
        x1, x1g, inv2 = _outproj(mix.reshape(bsz * seq, -1), w_out[l].astype(BF16), x2d, norm2[l])
        hid = _up(x1g, w1[l].astype(BF16), inv2)
        x2d = _down(hid, w2[l].astype(BF16), x1)
    return x2d.reshape(bsz, seq, d)
```

```python
import functools
import math

import jax
import jax.numpy as jnp
import numpy as np
from jax import lax
from jax.experimental import pallas as pl
from jax.experimental.pallas import tpu as pltpu

HEAD_DIM = 128
N_Q_HEADS = 16
N_KV_HEADS = 4
GQA_GROUP = N_Q_HEADS // N_KV_HEADS
ATTN_W = N_Q_HEADS * HEAD_DIM
KV_W = N_KV_HEADS * HEAD_DIM
WINDOW = 128
BLOCK = 128
N_BUCKETS = 32
MAX_DISTANCE = 128
GMLP_HEADS = 16
GMLP_W = GMLP_HEADS * HEAD_DIM
EPS = 1e-6
NEG = -1e30
LOG2E = math.log2(math.e)

F32 = jnp.float32
BF16 = jnp.bfloat16

V7X_VMEM_LIMIT_BYTES = 60 * 1024 * 1024


def _params(sem):
    return pltpu.CompilerParams(dimension_semantics=sem,
                                vmem_limit_bytes=V7X_VMEM_LIMIT_BYTES)


def _prep_kernel(x_ref, g_ref, xg_ref, inv_ref):
    x = x_ref[...]
    d = x.shape[-1]
    ss = jnp.sum(x * x, axis=-1, keepdims=True)
    inv_ref[...] = lax.rsqrt(ss * (1.0 / d) + EPS)
    xg_ref[...] = (x * g_ref[...]).astype(BF16)


def _prep(x2d, gain, rows=256):
    m, d = x2d.shape
    return pl.pallas_call(
        _prep_kernel,
        grid=(m // rows,),
        in_specs=[pl.BlockSpec((rows, d), lambda i: (i, 0)),
                  pl.BlockSpec((1, d), lambda i: (0, 0))],
        out_specs=[pl.BlockSpec((rows, d), lambda i: (i, 0)),
                   pl.BlockSpec((rows, 1), lambda i: (i, 0))],
        out_shape=[jax.ShapeDtypeStruct((m, d), BF16),
                   jax.ShapeDtypeStruct((m, 1), F32)],
        compiler_params=_params(("arbitrary",)),
        name="prep_norm",
    )(x2d, gain.reshape(1, d))


def _inproj_kernel(a_ref, b_ref, inv_ref, o_ref):
    acc = jnp.dot(a_ref[...], b_ref[...], preferred_element_type=F32)
    o_ref[...] = (acc * inv_ref[...]).astype(o_ref.dtype)


def _inproj(a, b, inv, tm=1024, tn=1024):
    m, k = a.shape
    n = b.shape[1]
    return pl.pallas_call(
        _inproj_kernel,
        grid=(m // tm, n // tn),
        in_specs=[pl.BlockSpec((tm, k), lambda i, j: (i, 0)),
                  pl.BlockSpec((k, tn), lambda i, j: (0, j)),
                  pl.BlockSpec((tm, 1), lambda i, j: (i, 0))],
        out_specs=pl.BlockSpec((tm, tn), lambda i, j: (i, j)),
        out_shape=jax.ShapeDtypeStruct((m, n), BF16),
        compiler_params=_params(("arbitrary", "arbitrary")),
        name="in_proj",
    )(a, b, inv)


def _up_kernel(a_ref, b_ref, inv_ref, o_ref):
    acc = jnp.dot(a_ref[...], b_ref[...], preferred_element_type=F32)
    r = jnp.maximum(acc * inv_ref[...], 0.0)
    o_ref[...] = (r * r).astype(o_ref.dtype)


def _up(a, b, inv, tm=1024, tn=1024):
    m, k = a.shape
    n = b.shape[1]
    return pl.pallas_call(
        _up_kernel,
        grid=(m // tm, n // tn),
        in_specs=[pl.BlockSpec((tm, k), lambda i, j: (i, 0)),
                  pl.BlockSpec((k, tn), lambda i, j: (0, j)),
                  pl.BlockSpec((tm, 1), lambda i, j: (i, 0))],
        out_specs=pl.BlockSpec((tm, tn), lambda i, j: (i, j)),
        out_shape=jax.ShapeDtypeStruct((m, n), BF16),
        compiler_params=_params(("arbitrary", "arbitrary")),
        name="mlp_up",
    )(a, b, inv)


def _outproj_kernel(a_ref, b_ref, x_ref, g_ref, x1_ref, x1g_ref, inv_ref, ss_ref, *, d_model):
    j = pl.program_id(1)
    acc = jnp.dot(a_ref[...], b_ref[...], preferred_element_type=F32)
    x1 = x_ref[...] + acc
    x1_ref[...] = x1
    x1g_ref[...] = (x1 * g_ref[...]).astype(BF16)
    part = jnp.sum(x1 * x1, axis=-1, keepdims=True)

    @pl.when(j == 0)
    def _():
        ss_ref[...] = part

    @pl.when(j > 0)
    def _():
        ss_ref[...] += part

    @pl.when(j == pl.num_programs(1) - 1)
    def _():
        inv_ref[...] = lax.rsqrt(ss_ref[...] * (1.0 / d_model) + EPS)


def _outproj(a, b, x2d, gain, tm=1024, tn=512):
    m, k = a.shape
    n = b.shape[1]
    return pl.pallas_call(
        functools.partial(_outproj_kernel, d_model=n),
        grid=(m // tm, n // tn),
        in_specs=[pl.BlockSpec((tm, k), lambda i, j: (i, 0)),
                  pl.BlockSpec((k, tn), lambda i, j: (0, j)),
                  pl.BlockSpec((tm, tn), lambda i, j: (i, j)),
                  pl.BlockSpec((1, tn), lambda i, j: (0, j))],
        out_specs=[pl.BlockSpec((tm, tn), lambda i, j: (i, j)),
                   pl.BlockSpec((tm, tn), lambda i, j: (i, j)),
                   pl.BlockSpec((tm, 1), lambda i, j: (i, 0))],
        out_shape=[jax.ShapeDtypeStruct((m, n), F32),
                   jax.ShapeDtypeStruct((m, n), BF16),
                   jax.ShapeDtypeStruct((m, 1), F32)],
        scratch_shapes=[pltpu.VMEM((tm, 1), F32)],
        compiler_params=_params(("arbitrary", "arbitrary")),
        name="out_proj",
    )(a, b, x2d, gain.reshape(1, n))


def _down_kernel(a_ref, b_ref, x1_ref, o_ref):
    @pl.when(pl.program_id(2) == 0)
    def _():
        o_ref[...] = x1_ref[...]

    o_ref[...] += jnp.dot(a_ref[...], b_ref[...], preferred_element_type=F32)


def _down(a, b, x1, tm=1024, tn=1024, tk=4096):
    m, k = a.shape
    n = b.shape[1]
    return pl.pallas_call(
        _down_kernel,
        grid=(m // tm, n // tn, k // tk),
        in_specs=[pl.BlockSpec((tm, tk), lambda i, j, kk: (i, kk)),
                  pl.BlockSpec((tk, tn), lambda i, j, kk: (kk, j)),
                  pl.BlockSpec((tm, tn), lambda i, j, kk: (i, j))],
        out_specs=pl.BlockSpec((tm, tn), lambda i, j, kk: (i, j)),
        out_shape=jax.ShapeDtypeStruct((m, n), F32),
        compiler_params=_params(("arbitrary", "arbitrary", "arbitrary")),
        name="mlp_down",
    )(a, b, x1)


def _t5_bucket(rel):
    nb = N_BUCKETS // 2
    max_exact = nb // 2
    ret = (rel > 0).astype(np.int32) * nb
    n = np.abs(rel)
    large = max_exact + (np.log(np.maximum(n, 1).astype(np.float32) / max_exact)
                         / math.log(MAX_DISTANCE / max_exact) * (nb - max_exact)).astype(np.int32)
    large = np.minimum(large, nb - 1)
    return ret + np.where(n < max_exact, n, large)


def _bucket_tables():
    a = np.arange(BLOCK)[:, None]
    s = np.arange(3 * BLOCK)[None, :]
    rel = s - BLOCK - a
    bucket = _t5_bucket(rel).astype(np.int32)
    in_window = np.abs(rel) <= WINDOW
    col = np.broadcast_to(s, rel.shape)
    first = in_window & (col >= BLOCK)
    middle = in_window
    last = in_window & (col < 2 * BLOCK)
    return np.stack([np.where(v, bucket, -1) for v in (first, middle, last)]).astype(np.int32)


def _bias_kernel(rb_ref, bk_ref, o_ref):
    h = pl.program_id(1)
    bk = bk_ref[...]
    acc = jnp.full(bk.shape, NEG, F32)
    for b in range(N_BUCKETS):
        acc = jnp.where(bk == b, rb_ref[b, h] * LOG2E, acc)
    o_ref[...] = acc


def _bias_tables(rel_bias):
    buckets = jnp.asarray(_bucket_tables())
    return pl.pallas_call(
        _bias_kernel,
        grid=(3, N_Q_HEADS),
        in_specs=[pl.BlockSpec(memory_space=pltpu.SMEM),
                  pl.BlockSpec((None, BLOCK, 3 * BLOCK), lambda v, h: (v, 0, 0))],
        out_specs=pl.BlockSpec((None, None, BLOCK, 3 * BLOCK), lambda v, h: (v, h, 0, 0)),
        out_shape=jax.ShapeDtypeStruct((3, N_Q_HEADS, BLOCK, 3 * BLOCK), F32),
        compiler_params=_params(("arbitrary", "arbitrary")),
        name="bias_tables",
    )(rel_bias, buckets)


def _rms_scale(t):
    return lax.rsqrt(jnp.sum(t * t, axis=-1, keepdims=True) * (1.0 / t.shape[-1]) + EPS)


def _gelu_tanh(x):
    c = math.sqrt(2.0 / math.pi)
    inner = x * (c + (c * 0.044715) * (x * x))
    return (0.5 * x) * (1.0 + jnp.tanh(inner))


def _mix_kernel(sink_ref, q_ref, kp_ref, ko_ref, kn_ref, vp_ref, vo_ref, vn_ref,
                u0_ref, u1_ref, g0_ref, g1_ref, bias_ref, qg_ref, kg_ref, ag_ref,
                vg_ref, ws_ref, bt_ref, gg_ref, mix_ref, a_scr, g_scr):
    n = pl.program_id(1)
    nlast = pl.num_programs(1) - 1
    variant = jnp.where(n == 0, 0, jnp.where(n == nlast, 2, 1))
    kfold = kg_ref[...] * qg_ref[...] * (HEAD_DIM ** -0.5 * LOG2E)

    kband = jnp.concatenate([kp_ref[...], ko_ref[...], kn_ref[...]], axis=0).astype(F32)
    vband = jnp.concatenate([vp_ref[...], vo_ref[...], vn_ref[...]], axis=0)
    for kh in range(N_KV_HEADS):
        k = kband[:, kh * HEAD_DIM:(kh + 1) * HEAD_DIM]
        kn = (k * _rms_scale(k) * kfold).astype(BF16)
        v = vband[:, kh * HEAD_DIM:(kh + 1) * HEAD_DIM]
        qs = []
        for g in range(GQA_GROUP):
            h = kh * GQA_GROUP + g
            qh = q_ref[:, h * HEAD_DIM:(h + 1) * HEAD_DIM].astype(F32)
            qs.append((qh * _rms_scale(qh)).astype(BF16))
        qstack = jnp.concatenate(qs, axis=0)
        s = lax.dot_general(qstack, kn, (((1,), (1,)), ((), ())),
                            preferred_element_type=F32)
        ps, rs = [], []
        for g in range(GQA_GROUP):
            h = kh * GQA_GROUP + g
            sink = sink_ref[h] * LOG2E
            sg = s[g * BLOCK:(g + 1) * BLOCK] + bias_ref[variant, h]
            m = jnp.maximum(jnp.max(sg, axis=-1, keepdims=True), sink)
            p = jnp.exp2(sg - m)
            rs.append(1.0 / (jnp.sum(p, axis=-1, keepdims=True) + jnp.exp2(sink - m)))
            ps.append(p.astype(BF16))
        pstack = jnp.concatenate(ps, axis=0)
        o = jnp.dot(pstack, v, preferred_element_type=F32)
        for g in range(GQA_GROUP):
            h = kh * GQA_GROUP + g
            a_scr[:, h * HEAD_DIM:(h + 1) * HEAD_DIM] = o[g * BLOCK:(g + 1) * BLOCK] * rs[g]

    a = a_scr[...]
    mix_ref[:, :ATTN_W] = (a * _rms_scale(a) * ag_ref[...]).astype(mix_ref.dtype)

    u = jnp.concatenate([u0_ref[...], u1_ref[...]], axis=1).astype(F32)
    gv = jnp.concatenate([g0_ref[...], g1_ref[...]], axis=1).astype(F32)
    gu = _gelu_tanh(u)
    gv = _gelu_tanh(gv)
    vn_ = (gv * _rms_scale(gv) * vg_ref[...]).astype(BF16)
    for h in range(GMLP_HEADS):
        sl = slice(h * HEAD_DIM, (h + 1) * HEAD_DIM)
        sv = jnp.dot(ws_ref[h], vn_[:, sl], preferred_element_type=F32) + bt_ref[:, h:h + 1]
        g_scr[:, sl] = gu[:, sl] * sv
    gg = g_scr[...]
    mix_ref[:, ATTN_W:] = (gg * _rms_scale(gg) * gg_ref[...]).astype(mix_ref.dtype)


def _mix(z3, bias, sink, q_gain, k_gain, attn_out_gain, v_gain, w_s, b_t, gmlp_out_gain):
    bsz, seq, _ = z3.shape
    nblk = seq // BLOCK
    assert nblk >= 2
    kcol = ATTN_W // KV_W
    vcol = kcol + 1
    ucol = (ATTN_W + 2 * KV_W) // 1024
    gcol = ucol + 2

    def zspec(width, col, shift=0):
        def imap(b, n):
            return (b, jnp.clip(n + shift, 0, nblk - 1), col)
        return pl.BlockSpec((None, BLOCK, width), imap)

    def full(shape):
        return pl.BlockSpec(shape, lambda b, n: (0,) * len(shape))

    in_specs = [
        pl.BlockSpec(memory_space=pltpu.SMEM),
        zspec(ATTN_W, 0),
        zspec(KV_W, kcol, -1), zspec(KV_W, kcol), zspec(KV_W, kcol, 1),
        zspec(KV_W, vcol, -1), zspec(KV_W, vcol), zspec(KV_W, vcol, 1),
        zspec(1024, ucol), zspec(1024, ucol + 1),
        zspec(1024, gcol), zspec(1024, gcol + 1),
        pl.BlockSpec(bias.shape, lambda b, n: (0, 0, 0, 0), pipeline_mode=pl.Buffered(1)),
        full((1, HEAD_DIM)), full((1, HEAD_DIM)), full((1, ATTN_W)),
        full((1, GMLP_W)), full(w_s.shape), full(b_t.shape), full((1, GMLP_W)),
    ]
    return pl.pallas_call(
        _mix_kernel,
        grid=(bsz, nblk),
        in_specs=in_specs,
        out_specs=pl.BlockSpec((None, BLOCK, ATTN_W + GMLP_W), lambda b, n: (b, n, 0)),
        out_shape=jax.ShapeDtypeStruct((bsz, seq, ATTN_W + GMLP_W), BF16),
        scratch_shapes=[pltpu.VMEM((BLOCK, ATTN_W), F32), pltpu.VMEM((BLOCK, GMLP_W), F32)],
        compiler_params=_params(("arbitrary", "arbitrary")),
        name="mix",
    )(sink, z3, z3, z3, z3, z3, z3, z3, z3, z3, z3, z3, bias,
      q_gain.reshape(1, HEAD_DIM), k_gain.reshape(1, HEAD_DIM), attn_out_gain.reshape(1, ATTN_W),
      v_gain.reshape(1, GMLP_W), w_s, b_t, gmlp_out_gain.reshape(1, GMLP_W))


def kernel(x, norm1, w_in, q_gain, k_gain, rel_bias, attn_sink, attn_out_gain,
           gmlp_v_gain, gmlp_w_s, gmlp_b_s, gmlp_out_gain, w_out, norm2, w1, w2):
    bsz, seq, d = x.shape
    depth = norm1.shape[0]
    bias = _bias_tables(rel_bias)
    x2d = x.reshape(bsz * seq, d)
    for l in range(depth):
        xg, inv1 = _prep(x2d, norm1[l])
        z = _inproj(xg, w_in[l].astype(BF16), inv1)
        mix = _mix(z.reshape(bsz, seq, -1), bias, attn_sink[l], q_gain[l], k_gain[l],
                   attn_out_gain[l], gmlp_v_gain[l], gmlp_w_s[l].astype(BF16),
                   gmlp_b_s[l].T, gmlp_out_gain[l])
        x1, x1g, inv2 = _outproj(mix.reshape(bsz * seq, -1), w_out[l].astype(BF16), x2d, norm2[l])
        hid = _up(x1g, w1[l].astype(BF16), inv2)
        x2d = _down(hid, w2[l].astype(BF16), x1)
    return x2d.reshape(bsz, seq, d)
```

```python
import functools
import math

import jax
import jax.numpy as jnp
import numpy as np
from jax import lax
from jax.experimental import pallas as pl
from jax.experimental.pallas import tpu as pltpu

HEAD_DIM = 128
N_Q_HEADS = 16
N_KV_HEADS = 4
GQA_GROUP = N_Q_HEADS // N_KV_HEADS
ATTN_W = N_Q_HEADS * HEAD_DIM
KV_W = N_KV_HEADS * HEAD_DIM
WINDOW = 128
BLOCK = 128
N_BUCKETS = 32
MAX_DISTANCE = 128
GMLP_HEADS = 16
GMLP_W = GMLP_HEADS * HEAD_DIM
EPS = 1e-6
NEG = -1e30
LOG2E = math.log2(math.e)

F32 = jnp.float32
BF16 = jnp.bfloat16

V7X_VMEM_LIMIT_BYTES = 60 * 1024 * 1024


def _params(sem):
    return pltpu.CompilerParams(dimension_semantics=sem,
                                vmem_limit_bytes=V7X_VMEM_LIMIT_BYTES)


def _prep_kernel(x_ref, g_ref, xg_ref, inv_ref):
    x = x_ref[...]
    d = x.shape[-1]
    ss = jnp.sum(x * x, axis=-1, keepdims=True)
    inv_ref[...] = lax.rsqrt(ss * (1.0 / d) + EPS)
    xg_ref[...] = (x * g_ref[...]).astype(BF16)


def _prep(x2d, gain, rows=256):
    m, d = x2d.shape
    return pl.pallas_call(
        _prep_kernel,
        grid=(m // rows,),
        in_specs=[pl.BlockSpec((rows, d), lambda i: (i, 0)),
                  pl.BlockSpec((1, d), lambda i: (0, 0))],
        out_specs=[pl.BlockSpec((rows, d), lambda i: (i, 0)),
                   pl.BlockSpec((rows, 1), lambda i: (i, 0))],
        out_shape=[jax.ShapeDtypeStruct((m, d), BF16),
                   jax.ShapeDtypeStruct((m, 1), F32)],
        compiler_params=_params(("arbitrary",)),
        name="prep_norm",
    )(x2d, gain.reshape(1, d))


FIRST_TN = 512


def _proj_kernel(a_ref, w_ref, *refs, body, n_in, n_prev, n_out, cast_w):
    ins = refs[:n_in]
    outs = refs[n_in + n_prev:n_in + n_prev + n_out]
    rest = refs[n_in + n_prev + n_out:]
    if cast_w:
        wb_ref, rest = rest[0], rest[1:]
        w = w_ref[...].astype(BF16)
        wb_ref[...] = w
    else:
        w = w_ref[...]
    acc = jnp.dot(a_ref[...], w, preferred_element_type=F32)
    body(acc, ins, outs, rest)


def _project(body, name, a, w, ins, outs, *, tm, tn, prev=None, scratch=()):
    m, k = a.shape
    n = w.shape[1]
    first = prev is None
    ioff = 0 if first else 1
    ni = 1 if first else m // tm - 1

    def spec(kind):
        if kind == "row":
            return pl.BlockSpec((tm, 1), lambda i, j: (i + ioff, 0))
        if kind == "col":
            return pl.BlockSpec((1, tn), lambda i, j: (0, j))
        return pl.BlockSpec((tm, tn), lambda i, j: (i + ioff, j))

    a_mode = dict(pipeline_mode=pl.Buffered(1)) if first else {}
    in_specs = [pl.BlockSpec((tm, k), lambda i, j: (i + ioff, 0), **a_mode),
                pl.BlockSpec((k, tn), lambda i, j: (0, j))] + [spec(kind) for _, kind in ins]
    args = [a, w] + [arr for arr, _ in ins]
    out_specs = [spec(kind) for _, kind in outs]
    out_shape = [sds for sds, _ in outs]
    aliases = {}
    if first:
        out_specs.append(pl.BlockSpec((k, tn), lambda i, j: (0, j)))
        out_shape.append(jax.ShapeDtypeStruct((k, n), BF16))
    else:
        for idx, p in enumerate(prev):
            aliases[len(args)] = idx
            in_specs.append(pl.BlockSpec(memory_space=pl.ANY))
            args.append(p)
    return pl.pallas_call(
        functools.partial(_proj_kernel, body=body, n_in=len(ins), n_prev=len(aliases),
                          n_out=len(outs), cast_w=first),
        grid=(ni, n // tn),
        in_specs=in_specs,
        out_specs=out_specs,
        out_shape=out_shape,
        input_output_aliases=aliases,
        scratch_shapes=list(scratch),
        compiler_params=_params(("arbitrary", "arbitrary")),
        name=name + ("_first" if first else "_rest"),
    )(*args)


def _two_pass(body, name, a, w_f32, ins, outs, *, tm, tn, scratch=()):
    *part, wb = _project(body, name, a, w_f32, ins, outs, tm=tm, tn=FIRST_TN, scratch=scratch)
    return _project(body, name, a, wb, ins, outs, tm=tm, tn=tn, prev=part, scratch=scratch)


def _scale_body(acc, ins, outs, scratch):
    (inv_ref,), (o_ref,) = ins, outs
    o_ref[...] = (acc * inv_ref[...]).astype(o_ref.dtype)


def _relu2_body(acc, ins, outs, scratch):
    (inv_ref,), (o_ref,) = ins, outs
    r = jnp.maximum(acc * inv_ref[...], 0.0)
    o_ref[...] = (r * r).astype(o_ref.dtype)


def _residual_norm_body(acc, ins, outs, scratch, *, d_model):
    (x_ref, g_ref), (x1_ref, x1g_ref, inv_ref), (ss_ref,) = ins, outs, scratch
    j = pl.program_id(1)
    x1 = x_ref[...] + acc
    x1_ref[...] = x1
    x1g_ref[...] = (x1 * g_ref[...]).astype(BF16)
    part = jnp.sum(x1 * x1, axis=-1, keepdims=True)

    @pl.when(j == 0)
    def _():
        ss_ref[...] = part

    @pl.when(j > 0)
    def _():
        ss_ref[...] += part

    @pl.when(j == pl.num_programs(1) - 1)
    def _():
        inv_ref[...] = lax.rsqrt(ss_ref[...] * (1.0 / d_model) + EPS)


def _inproj(xg, w_in, inv, tm=1024, tn=1024):
    m, n = xg.shape[0], w_in.shape[1]
    (z,) = _two_pass(_scale_body, "in_proj", xg, w_in, [(inv, "row")],
                     [(jax.ShapeDtypeStruct((m, n), BF16), "tile")], tm=tm, tn=tn)
    return z


def _up(x1g, w1, inv, tm=1024, tn=1024):
    m, n = x1g.shape[0], w1.shape[1]
    (hid,) = _two_pass(_relu2_body, "mlp_up", x1g, w1, [(inv, "row")],
                       [(jax.ShapeDtypeStruct((m, n), BF16), "tile")], tm=tm, tn=tn)
    return hid


def _outproj(mix, w_out, x2d, gain, tm=1024, tn=512):
    m, n = x2d.shape
    outs = [(jax.ShapeDtypeStruct((m, n), F32), "tile"),
            (jax.ShapeDtypeStruct((m, n), BF16), "tile"),
            (jax.ShapeDtypeStruct((m, 1), F32), "row")]
    return _two_pass(functools.partial(_residual_norm_body, d_model=n), "out_proj", mix, w_out,
                     [(x2d, "tile"), (gain.reshape(1, n), "col")], outs, tm=tm, tn=tn,
                     scratch=[pltpu.VMEM((tm, 1), F32)])


def _down_kernel(a_ref, w_ref, x1_ref, *refs, cast_w):
    o_ref = refs[0] if cast_w else refs[1]

    @pl.when(pl.program_id(2) == 0)
    def _():
        o_ref[...] = x1_ref[...]

    if cast_w:
        wb_ref = refs[1]
        w = w_ref[...].astype(BF16)
        wb_ref[...] = w
    else:
        w = w_ref[...]
    o_ref[...] += jnp.dot(a_ref[...], w, preferred_element_type=F32)


def _down_call(a, w, x1, *, tm, tn, tk, prev=None):
    m, k = a.shape
    n = w.shape[1]
    first = prev is None
    ioff = 0 if first else 1
    ni = 1 if first else m // tm - 1
    in_specs = [pl.BlockSpec((tm, tk), lambda i, j, kk: (i + ioff, kk)),
                pl.BlockSpec((tk, tn), lambda i, j, kk: (kk, j)),
                pl.BlockSpec((tm, tn), lambda i, j, kk: (i + ioff, j))]
    args = [a, w, x1]
    out_specs = [pl.BlockSpec((tm, tn), lambda i, j, kk: (i + ioff, j))]
    out_shape = [jax.ShapeDtypeStruct((m, n), F32)]
    aliases = {}
    if first:
        out_specs.append(pl.BlockSpec((tk, tn), lambda i, j, kk: (kk, j)))
        out_shape.append(jax.ShapeDtypeStruct((k, n), BF16))
    else:
        aliases[len(args)] = 0
        in_specs.append(pl.BlockSpec(memory_space=pl.ANY))
        args.append(prev)
    return pl.pallas_call(
        functools.partial(_down_kernel, cast_w=first),
        grid=(ni, n // tn, k // tk),
        in_specs=in_specs,
        out_specs=out_specs,
        out_shape=out_shape,
        input_output_aliases=aliases,
        compiler_params=_params(("arbitrary", "arbitrary", "arbitrary")),
        name="mlp_down" + ("_first" if first else "_rest"),
    )(*args)


def _down(hid, w2, x1, tm=1024, tn=1024, tk=4096):
    part, wb = _down_call(hid, w2, x1, tm=tm, tn=tn, tk=tk // 2)
    (y,) = _down_call(hid, wb, x1, tm=tm, tn=tn, tk=tk, prev=part)
    return y


def _t5_bucket(rel):
    nb = N_BUCKETS // 2
    max_exact = nb // 2
    ret = (rel > 0).astype(np.int32) * nb
    n = np.abs(rel)
    large = max_exact + (np.log(np.maximum(n, 1).astype(np.float32) / max_exact)
                         / math.log(MAX_DISTANCE / max_exact) * (nb - max_exact)).astype(np.int32)
    large = np.minimum(large, nb - 1)
    return ret + np.where(n < max_exact, n, large)


def _bucket_tables():
    a = np.arange(BLOCK)[:, None]
    s = np.arange(3 * BLOCK)[None, :]
    rel = s - BLOCK - a
    bucket = _t5_bucket(rel).astype(np.int32)
    in_window = np.abs(rel) <= WINDOW
    col = np.broadcast_to(s, rel.shape)
    first = in_window & (col >= BLOCK)
    middle = in_window
    last = in_window & (col < 2 * BLOCK)
    return np.stack([np.where(v, bucket, -1) for v in (first, middle, last)]).astype(np.int32)


def _bias_kernel(rb_ref, bk_ref, o_ref):
    h = pl.program_id(1)
    bk = bk_ref[...]
    acc = jnp.full(bk.shape, NEG, F32)
    for b in range(N_BUCKETS):
        acc = jnp.where(bk == b, rb_ref[b, h] * LOG2E, acc)
    o_ref[...] = acc


def _bias_tables(rel_bias):
    buckets = jnp.asarray(_bucket_tables())
    return pl.pallas_call(
        _bias_kernel,
        grid=(3, N_Q_HEADS),
        in_specs=[pl.BlockSpec(memory_space=pltpu.SMEM),
                  pl.BlockSpec((None, BLOCK, 3 * BLOCK), lambda v, h: (v, 0, 0))],
        out_specs=pl.BlockSpec((None, None, BLOCK, 3 * BLOCK), lambda v, h: (v, h, 0, 0)),
        out_shape=jax.ShapeDtypeStruct((3, N_Q_HEADS, BLOCK, 3 * BLOCK), F32),
        compiler_params=_params(("arbitrary", "arbitrary")),
        name="bias_tables",
    )(rel_bias, buckets)


def _rms_scale(t):
    return lax.rsqrt(jnp.sum(t * t, axis=-1, keepdims=True) * (1.0 / t.shape[-1]) + EPS)


def _gelu_tanh(x):
    c = math.sqrt(2.0 / math.pi)
    inner = x * (c + (c * 0.044715) * (x * x))
    return (0.5 * x) * (1.0 + jnp.tanh(inner))


def _mix_kernel(sink_ref, q_ref, kp_ref, ko_ref, kn_ref, vp_ref, vo_ref, vn_ref,
                u0_ref, u1_ref, g0_ref, g1_ref, bias_ref, qg_ref, kg_ref, ag_ref,
                vg_ref, ws_ref, bt_ref, gg_ref, mix_ref, a_scr, g_scr):
    n = pl.program_id(1)
    nlast = pl.num_programs(1) - 1
    variant = jnp.where(n == 0, 0, jnp.where(n == nlast, 2, 1))
    kfold = kg_ref[...] * qg_ref[...] * (HEAD_DIM ** -0.5 * LOG2E)

    kband = jnp.concatenate([kp_ref[...], ko_ref[...], kn_ref[...]], axis=0).astype(F32)
    vband = jnp.concatenate([vp_ref[...], vo_ref[...], vn_ref[...]], axis=0)
    for kh in range(N_KV_HEADS):
        k = kband[:, kh * HEAD_DIM:(kh + 1) * HEAD_DIM]
        kn = (k * _rms_scale(k) * kfold).astype(BF16)
        v = vband[:, kh * HEAD_DIM:(kh + 1) * HEAD_DIM]
        qs = []
        for g in range(GQA_GROUP):
            h = kh * GQA_GROUP + g
            qh = q_ref[:, h * HEAD_DIM:(h + 1) * HEAD_DIM].astype(F32)
            qs.append((qh * _rms_scale(qh)).astype(BF16))
        qstack = jnp.concatenate(qs, axis=0)
        s = lax.dot_general(qstack, kn, (((1,), (1,)), ((), ())),
                            preferred_element_type=F32)
        ps, rs = [], []
        for g in range(GQA_GROUP):
            h = kh * GQA_GROUP + g
            sink = sink_ref[h] * LOG2E
            sg = s[g * BLOCK:(g + 1) * BLOCK] + bias_ref[variant, h]
            m = jnp.maximum(jnp.max(sg, axis=-1, keepdims=True), sink)
            p = jnp.exp2(sg - m)
            rs.append(1.0 / (jnp.sum(p, axis=-1, keepdims=True) + jnp.exp2(sink - m)))
            ps.append(p.astype(BF16))
        pstack = jnp.concatenate(ps, axis=0)
        o = jnp.dot(pstack, v, preferred_element_type=F32)
        for g in range(GQA_GROUP):
            h = kh * GQA_GROUP + g
            a_scr[:, h * HEAD_DIM:(h + 1) * HEAD_DIM] = o[g * BLOCK:(g + 1) * BLOCK] * rs[g]

    a = a_scr[...]
    mix_ref[:, :ATTN_W] = (a * _rms_scale(a) * ag_ref[...]).astype(mix_ref.dtype)

    u = jnp.concatenate([u0_ref[...], u1_ref[...]], axis=1).astype(F32)
    gv = jnp.concatenate([g0_ref[...], g1_ref[...]], axis=1).astype(F32)
    gu = _gelu_tanh(u)
    gv = _gelu_tanh(gv)
    vn_ = (gv * _rms_scale(gv) * vg_ref[...]).astype(BF16)
    for h in range(GMLP_HEADS):
        sl = slice(h * HEAD_DIM, (h + 1) * HEAD_DIM)
        sv = jnp.dot(ws_ref[h], vn_[:, sl], preferred_element_type=F32) + bt_ref[:, h:h + 1]
        g_scr[:, sl] = gu[:, sl] * sv
    gg = g_scr[...]
    mix_ref[:, ATTN_W:] = (gg * _rms_scale(gg) * gg_ref[...]).astype(mix_ref.dtype)


def _mix(z3, bias, sink, q_gain, k_gain, attn_out_gain, v_gain, w_s, b_t, gmlp_out_gain):
    bsz, seq, _ = z3.shape
    nblk = seq // BLOCK
    assert nblk >= 2
    kcol = ATTN_W // KV_W
    vcol = kcol + 1
    ucol = (ATTN_W + 2 * KV_W) // 1024
    gcol = ucol + 2

    def zspec(width, col, shift=0):
        def imap(b, n):
            return (b, jnp.clip(n + shift, 0, nblk - 1), col)
        return pl.BlockSpec((None, BLOCK, width), imap)

    def full(shape):
        return pl.BlockSpec(shape, lambda b, n: (0,) * len(shape))

    in_specs = [
        pl.BlockSpec(memory_space=pltpu.SMEM),
        zspec(ATTN_W, 0),
        zspec(KV_W, kcol, -1), zspec(KV_W, kcol), zspec(KV_W, kcol, 1),
        zspec(KV_W, vcol, -1), zspec(KV_W, vcol), zspec(KV_W, vcol, 1),
        zspec(1024, ucol), zspec(1024, ucol + 1),
        zspec(1024, gcol), zspec(1024, gcol + 1),
        pl.BlockSpec(bias.shape, lambda b, n: (0, 0, 0, 0), pipeline_mode=pl.Buffered(1)),
        full((1, HEAD_DIM)), full((1, HEAD_DIM)), full((1, ATTN_W)),
        full((1, GMLP_W)), full(w_s.shape), full(b_t.shape), full((1, GMLP_W)),
    ]
    return pl.pallas_call(
        _mix_kernel,
        grid=(bsz, nblk),
        in_specs=in_specs,
        out_specs=pl.BlockSpec((None, BLOCK, ATTN_W + GMLP_W), lambda b, n: (b, n, 0)),
        out_shape=jax.ShapeDtypeStruct((bsz, seq, ATTN_W + GMLP_W), BF16),
        scratch_shapes=[pltpu.VMEM((BLOCK, ATTN_W), F32), pltpu.VMEM((BLOCK, GMLP_W), F32)],
        compiler_params=_params(("arbitrary", "arbitrary")),
        name="mix",
    )(sink, z3, z3, z3, z3, z3, z3, z3, z3, z3, z3, z3, bias,
      q_gain.reshape(1, HEAD_DIM), k_gain.reshape(1, HEAD_DIM), attn_out_gain.reshape(1, ATTN_W),
      v_gain.reshape(1, GMLP_W), w_s, b_t, gmlp_out_gain.reshape(1, GMLP_W))


def kernel(x, norm1, w_in, q_gain, k_gain, rel_bias, attn_sink, attn_out_gain,
           gmlp_v_gain, gmlp_w_s, gmlp_b_s, gmlp_out_gain, w_out, norm2, w1, w2):
    bsz, seq, d = x.shape
    depth = norm1.shape[0]
    bias = _bias_tables(rel_bias)
    x2d = x.reshape(bsz * seq, d)
    for l in range(depth):
        xg, inv1 = _prep(x2d, norm1[l])
        z = _inproj(xg, w_in[l], inv1)
        mix = _mix(z.reshape(bsz, seq, -1), bias, attn_sink[l], q_gain[l], k_gain[l],
                   attn_out_gain[l], gmlp_v_gain[l], gmlp_w_s[l].astype(BF16),
                   gmlp_b_s[l].T, gmlp_out_gain[l])
        x1, x1g, inv2 = _outproj(mix.reshape(bsz * seq, -1), w_out[l], x2d, norm2[l])
        hid = _up(x1g, w1[l], inv2)
        x2d = _down(hid, w2[l], x1)
    return x2d.reshape(bsz, seq, d)
```

```python
import functools
import math

import jax
import jax.numpy as jnp
import numpy as np
from jax import lax
from jax.experimental import pallas as pl
from jax.experimental.pallas import tpu as pltpu

HEAD_DIM = 128
N_Q_HEADS = 16
N_KV_HEADS = 4
GQA_GROUP = N_Q_HEADS // N_KV_HEADS
ATTN_W = N_Q_HEADS * HEAD_DIM
KV_W = N_KV_HEADS * HEAD_DIM
WINDOW = 128
BLOCK = 128
N_BUCKETS = 32
MAX_DISTANCE = 128
GMLP_HEADS = 16
GMLP_W = GMLP_HEADS * HEAD_DIM
EPS = 1e-6
NEG = -1e30
LOG2E = math.log2(math.e)

F32 = jnp.float32
BF16 = jnp.bfloat16

V7X_VMEM_LIMIT_BYTES = 60 * 1024 * 1024


def _params(sem):
    return pltpu.CompilerParams(dimension_semantics=sem,
                                vmem_limit_bytes=V7X_VMEM_LIMIT_BYTES)


def _prep_kernel(x_ref, g_ref, xg_ref, inv_ref):
    x = x_ref[...]
    d = x.shape[-1]
    ss = jnp.sum(x * x, axis=-1, keepdims=True)
    inv_ref[...] = lax.rsqrt(ss * (1.0 / d) + EPS)
    xg_ref[...] = (x * g_ref[...]).astype(BF16)


def _prep(x2d, gain, n_rows, rows=256):
    m, d = n_rows, x2d.shape[1]
    return pl.pallas_call(
        _prep_kernel,
        grid=(m // rows,),
        in_specs=[pl.BlockSpec((rows, d), lambda i: (i, 0)),
                  pl.BlockSpec((1, d), lambda i: (0, 0))],
        out_specs=[pl.BlockSpec((rows, d), lambda i: (i, 0)),
                   pl.BlockSpec((rows, 1), lambda i: (i, 0))],
        out_shape=[jax.ShapeDtypeStruct((m, d), BF16),
                   jax.ShapeDtypeStruct((m, 1), F32)],
        compiler_params=_params(("arbitrary",)),
        name="prep_norm",
    )(x2d, gain.reshape(1, d))


FIRST_TN = 512


def _proj_kernel(a_ref, w_ref, *refs, body, n_in, n_prev, n_out, cast_w):
    ins = refs[:n_in]
    outs = refs[n_in + n_prev:n_in + n_prev + n_out]
    rest = refs[n_in + n_prev + n_out:]
    if cast_w:
        wb_ref, rest = rest[0], rest[1:]
        w = w_ref[...].astype(BF16)
        wb_ref[...] = w
    else:
        w = w_ref[...]
    acc = jnp.dot(a_ref[...], w, preferred_element_type=F32)
    body(acc, ins, outs, rest)


def _project(body, name, a, w, ins, outs, *, tm, tn, prev=None, scratch=()):
    m, k = a.shape
    n = w.shape[1]
    first = prev is None
    ioff = 0 if first else 1
    ni = 1 if first else m // tm - 1

    def spec(kind):
        if kind == "row":
            return pl.BlockSpec((tm, 1), lambda i, j: (i + ioff, 0))
        if kind == "col":
            return pl.BlockSpec((1, tn), lambda i, j: (0, j))
        return pl.BlockSpec((tm, tn), lambda i, j: (i + ioff, j))

    a_mode = dict(pipeline_mode=pl.Buffered(1)) if first else {}
    in_specs = [pl.BlockSpec((tm, k), lambda i, j: (i + ioff, 0), **a_mode),
                pl.BlockSpec((k, tn), lambda i, j: (0, j))] + [spec(kind) for _, kind in ins]
    args = [a, w] + [arr for arr, _ in ins]
    out_specs = [spec(kind) for _, kind in outs]
    out_shape = [sds for sds, _ in outs]
    aliases = {}
    if first:
        out_specs.append(pl.BlockSpec((k, tn), lambda i, j: (0, j)))
        out_shape.append(jax.ShapeDtypeStruct((k, n), BF16))
    else:
        for idx, p in enumerate(prev):
            aliases[len(args)] = idx
            in_specs.append(pl.BlockSpec(memory_space=pl.ANY))
            args.append(p)
    return pl.pallas_call(
        functools.partial(_proj_kernel, body=body, n_in=len(ins), n_prev=len(aliases),
                          n_out=len(outs), cast_w=first),
        grid=(ni, n // tn),
        in_specs=in_specs,
        out_specs=out_specs,
        out_shape=out_shape,
        input_output_aliases=aliases,
        scratch_shapes=list(scratch),
        compiler_params=_params(("arbitrary", "arbitrary")),
        name=name + ("_first" if first else "_rest"),
    )(*args)


def _two_pass(body, name, a, w_f32, ins, outs, *, tm, tn, scratch=()):
    *part, wb = _project(body, name, a, w_f32, ins, outs, tm=tm, tn=FIRST_TN, scratch=scratch)
    return _project(body, name, a, wb, ins, outs, tm=tm, tn=tn, prev=part, scratch=scratch)


def _scale_body(acc, ins, outs, scratch):
    (inv_ref,), (o_ref,) = ins, outs
    o_ref[...] = (acc * inv_ref[...]).astype(o_ref.dtype)


def _relu2_body(acc, ins, outs, scratch):
    (inv_ref,), (o_ref,) = ins, outs
    r = jnp.maximum(acc * inv_ref[...], 0.0)
    o_ref[...] = (r * r).astype(o_ref.dtype)


def _residual_norm_body(acc, ins, outs, scratch, *, d_model):
    (x_ref, g_ref), (x1_ref, x1g_ref, inv_ref), (ss_ref,) = ins, outs, scratch
    j = pl.program_id(1)
    x1 = x_ref[...] + acc
    x1_ref[...] = x1
    x1g_ref[...] = (x1 * g_ref[...]).astype(BF16)
    part = jnp.sum(x1 * x1, axis=-1, keepdims=True)

    @pl.when(j == 0)
    def _():
        ss_ref[...] = part

    @pl.when(j > 0)
    def _():
        ss_ref[...] += part

    @pl.when(j == pl.num_programs(1) - 1)
    def _():
        inv_ref[...] = lax.rsqrt(ss_ref[...] * (1.0 / d_model) + EPS)


PREP_ROWS = 256


def _inproj_rest_kernel(x_ref, g_ref, w_ref, z_prev_ref, o_ref, xg_a, xg_b, inv_a, inv_b,
                        *, n_row_blocks, prep_steps):
    del z_prev_ref
    r = pl.program_id(0)
    j = pl.program_id(1)

    def project(xg_in, inv_in):
        acc = jnp.dot(xg_in[...], w_ref[...], preferred_element_type=F32)
        o_ref[...] = (acc * inv_in[...]).astype(o_ref.dtype)

    def prep(xg_out, inv_out):
        x = x_ref[...]
        rows = pl.ds(pl.multiple_of(j * PREP_ROWS, PREP_ROWS), PREP_ROWS)
        ss = jnp.sum(x * x, axis=-1, keepdims=True)
        inv_out[rows, :] = lax.rsqrt(ss * (1.0 / x.shape[-1]) + EPS)
        xg_out[rows, :] = (x * g_ref[...]).astype(BF16)

    for parity, (bufs_in, bufs_out) in enumerate((((xg_a, inv_a), (xg_b, inv_b)),
                                                  ((xg_b, inv_b), (xg_a, inv_a)))):
        mine = lax.rem(r, 2) == parity
        has_next = r < n_row_blocks - 1
        prepping = has_next & (j < prep_steps)

        @pl.when(mine & (r == 0) & prepping)
        def _():
            prep(*bufs_out)

        @pl.when(mine & (r > 0) & prepping)
        def _():
            project(*bufs_in)
            prep(*bufs_out)

        @pl.when(mine & (r > 0) & jnp.logical_not(prepping))
        def _():
            project(*bufs_in)


def _inproj(x2d, gain, w_in, tm=1024, tn=1024):
    m, k = x2d.shape
    n = w_in.shape[1]
    ni, nj = m // tm, n // tn
    prep_steps = tm // PREP_ROWS
    assert prep_steps <= nj
    xg0, inv0 = _prep(x2d, gain, tm)
    z_part, wb = _project(_scale_body, "in_proj", xg0, w_in, [(inv0, "row")],
                          [(jax.ShapeDtypeStruct((m, n), BF16), "tile")], tm=tm, tn=FIRST_TN)

    def x_map(r, j):
        return (jnp.minimum(r + 1, ni - 1) * prep_steps + jnp.minimum(j, prep_steps - 1), 0)

    def col(r, j):
        return jnp.where(r == 0, 0, j)

    return pl.pallas_call(
        functools.partial(_inproj_rest_kernel, n_row_blocks=ni, prep_steps=prep_steps),
        grid=(ni, nj),
        in_specs=[pl.BlockSpec((PREP_ROWS, k), x_map),
                  pl.BlockSpec((1, k), lambda r, j: (0, 0)),
                  pl.BlockSpec((k, tn), lambda r, j: (0, col(r, j))),
                  pl.BlockSpec(memory_space=pl.ANY)],
        out_specs=pl.BlockSpec((tm, tn), lambda r, j: (jnp.maximum(r, 1), col(r, j))),
        out_shape=jax.ShapeDtypeStruct((m, n), BF16),
        input_output_aliases={3: 0},
        scratch_shapes=[pltpu.VMEM((tm, k), BF16), pltpu.VMEM((tm, k), BF16),
                        pltpu.VMEM((tm, 1), F32), pltpu.VMEM((tm, 1), F32)],
        compiler_params=_params(("arbitrary", "arbitrary")),
        name="in_proj_rest",
    )(x2d, gain.reshape(1, k), wb, z_part)


def _up(x1g, w1, inv, tm=1024, tn=1024):
    m, n = x1g.shape[0], w1.shape[1]
    (hid,) = _two_pass(_relu2_body, "mlp_up", x1g, w1, [(inv, "row")],
                       [(jax.ShapeDtypeStruct((m, n), BF16), "tile")], tm=tm, tn=tn)
    return hid


def _outproj(mix, w_out, x2d, gain, tm=1024, tn=512):
    m, n = x2d.shape
    outs = [(jax.ShapeDtypeStruct((m, n), F32), "tile"),
            (jax.ShapeDtypeStruct((m, n), BF16), "tile"),
            (jax.ShapeDtypeStruct((m, 1), F32), "row")]
    return _two_pass(functools.partial(_residual_norm_body, d_model=n), "out_proj", mix, w_out,
                     [(x2d, "tile"), (gain.reshape(1, n), "col")], outs, tm=tm, tn=tn,
                     scratch=[pltpu.VMEM((tm, 1), F32)])


def _down_kernel(a_ref, w_ref, x1_ref, *refs, cast_w):
    o_ref = refs[0] if cast_w else refs[1]

    @pl.when(pl.program_id(2) == 0)
    def _():
        o_ref[...] = x1_ref[...]

    if cast_w:
        wb_ref = refs[1]
        w = w_ref[...].astype(BF16)
        wb_ref[...] = w
    else:
        w = w_ref[...]
    o_ref[...] += jnp.dot(a_ref[...], w, preferred_element_type=F32)


def _down_call(a, w, x1, *, tm, tn, tk, prev=None):
    m, k = a.shape
    n = w.shape[1]
    first = prev is None
    ioff = 0 if first else 1
    ni = 1 if first else m // tm - 1
    in_specs = [pl.BlockSpec((tm, tk), lambda i, j, kk: (i + ioff, kk)),
                pl.BlockSpec((tk, tn), lambda i, j, kk: (kk, j)),
                pl.BlockSpec((tm, tn), lambda i, j, kk: (i + ioff, j))]
    args = [a, w, x1]
    out_specs = [pl.BlockSpec((tm, tn), lambda i, j, kk: (i + ioff, j))]
    out_shape = [jax.ShapeDtypeStruct((m, n), F32)]
    aliases = {}
    if first:
        out_specs.append(pl.BlockSpec((tk, tn), lambda i, j, kk: (kk, j)))
        out_shape.append(jax.ShapeDtypeStruct((k, n), BF16))
    else:
        aliases[len(args)] = 0
        in_specs.append(pl.BlockSpec(memory_space=pl.ANY))
        args.append(prev)
    return pl.pallas_call(
        functools.partial(_down_kernel, cast_w=first),
        grid=(ni, n // tn, k // tk),
        in_specs=in_specs,
        out_specs=out_specs,
        out_shape=out_shape,
        input_output_aliases=aliases,
        compiler_params=_params(("arbitrary", "arbitrary", "arbitrary")),
        name="mlp_down" + ("_first" if first else "_rest"),
    )(*args)


def _down(hid, w2, x1, tm=1024, tn=1024, tk=4096):
    part, wb = _down_call(hid, w2, x1, tm=tm, tn=tn, tk=tk // 2)
    (y,) = _down_call(hid, wb, x1, tm=tm, tn=tn, tk=tk, prev=part)
    return y


def _t5_bucket(rel):
    nb = N_BUCKETS // 2
    max_exact = nb // 2
    ret = (rel > 0).astype(np.int32) * nb
    n = np.abs(rel)
    large = max_exact + (np.log(np.maximum(n, 1).astype(np.float32) / max_exact)
                         / math.log(MAX_DISTANCE / max_exact) * (nb - max_exact)).astype(np.int32)
    large = np.minimum(large, nb - 1)
    return ret + np.where(n < max_exact, n, large)


def _bucket_tables():
    a = np.arange(BLOCK)[:, None]
    s = np.arange(3 * BLOCK)[None, :]
    rel = s - BLOCK - a
    bucket = _t5_bucket(rel).astype(np.int32)
    in_window = np.abs(rel) <= WINDOW
    col = np.broadcast_to(s, rel.shape)
    first = in_window & (col >= BLOCK)
    middle = in_window
    last = in_window & (col < 2 * BLOCK)
    return np.stack([np.where(v, bucket, -1) for v in (first, middle, last)]).astype(np.int32)


def _bias_kernel(rb_ref, bk_ref, o_ref):
    h = pl.program_id(1)
    bk = bk_ref[...]
    acc = jnp.full(bk.shape, NEG, F32)
    for b in range(N_BUCKETS):
        acc = jnp.where(bk == b, rb_ref[b, h] * LOG2E, acc)
    o_ref[...] = acc


def _bias_tables(rel_bias):
    buckets = jnp.asarray(_bucket_tables())
    return pl.pallas_call(
        _bias_kernel,
        grid=(3, N_Q_HEADS),
        in_specs=[pl.BlockSpec(memory_space=pltpu.SMEM),
                  pl.BlockSpec((None, BLOCK, 3 * BLOCK), lambda v, h: (v, 0, 0))],
        out_specs=pl.BlockSpec((None, None, BLOCK, 3 * BLOCK), lambda v, h: (v, h, 0, 0)),
        out_shape=jax.ShapeDtypeStruct((3, N_Q_HEADS, BLOCK, 3 * BLOCK), F32),
        compiler_params=_params(("arbitrary", "arbitrary")),
        name="bias_tables",
    )(rel_bias, buckets)


def _rms_scale(t):
    return lax.rsqrt(jnp.sum(t * t, axis=-1, keepdims=True) * (1.0 / t.shape[-1]) + EPS)


def _gelu_tanh(x):
    c = math.sqrt(2.0 / math.pi)
    inner = x * (c + (c * 0.044715) * (x * x))
    return (0.5 * x) * (1.0 + jnp.tanh(inner))


def _mix_kernel(sink_ref, q_ref, kp_ref, ko_ref, kn_ref, vp_ref, vo_ref, vn_ref,
                u0_ref, u1_ref, g0_ref, g1_ref, bias_ref, qg_ref, kg_ref, ag_ref,
                vg_ref, ws_ref, bt_ref, gg_ref, mix_ref, a_scr, g_scr):
    n = pl.program_id(1)
    nlast = pl.num_programs(1) - 1
    variant = jnp.where(n == 0, 0, jnp.where(n == nlast, 2, 1))
    kfold = kg_ref[...] * qg_ref[...] * (HEAD_DIM ** -0.5 * LOG2E)

    kband = jnp.concatenate([kp_ref[...], ko_ref[...], kn_ref[...]], axis=0).astype(F32)
    vband = jnp.concatenate([vp_ref[...], vo_ref[...], vn_ref[...]], axis=0)
    for kh in range(N_KV_HEADS):
        k = kband[:, kh * HEAD_DIM:(kh + 1) * HEAD_DIM]
        kn = (k * _rms_scale(k) * kfold).astype(BF16)
        v = vband[:, kh * HEAD_DIM:(kh + 1) * HEAD_DIM]
        qs = []
        for g in range(GQA_GROUP):
            h = kh * GQA_GROUP + g
            qh = q_ref[:, h * HEAD_DIM:(h + 1) * HEAD_DIM].astype(F32)
            qs.append((qh * _rms_scale(qh)).astype(BF16))
        qstack = jnp.concatenate(qs, axis=0)
        s = lax.dot_general(qstack, kn, (((1,), (1,)), ((), ())),
                            preferred_element_type=F32)
        ps, rs = [], []
        for g in range(GQA_GROUP):
            h = kh * GQA_GROUP + g
            sink = sink_ref[h] * LOG2E
            sg = s[g * BLOCK:(g + 1) * BLOCK] + bias_ref[variant, h]
            m = jnp.maximum(jnp.max(sg, axis=-1, keepdims=True), sink)
            p = jnp.exp2(sg - m)
            rs.append(1.0 / (jnp.sum(p, axis=-1, keepdims=True) + jnp.exp2(sink - m)))
            ps.append(p.astype(BF16))
        pstack = jnp.concatenate(ps, axis=0)
        o = jnp.dot(pstack, v, preferred_element_type=F32)
        for g in range(GQA_GROUP):
            h = kh * GQA_GROUP + g
            a_scr[:, h * HEAD_DIM:(h + 1) * HEAD_DIM] = o[g * BLOCK:(g + 1) * BLOCK] * rs[g]

    a = a_scr[...]
    mix_ref[:, :ATTN_W] = (a * _rms_scale(a) * ag_ref[...]).astype(mix_ref.dtype)

    u = jnp.concatenate([u0_ref[...], u1_ref[...]], axis=1).astype(F32)
    gv = jnp.concatenate([g0_ref[...], g1_ref[...]], axis=1).astype(F32)
    gu = _gelu_tanh(u)
    gv = _gelu_tanh(gv)
    vn_ = (gv * _rms_scale(gv) * vg_ref[...]).astype(BF16)
    for h in range(GMLP_HEADS):
        sl = slice(h * HEAD_DIM, (h + 1) * HEAD_DIM)
        sv = jnp.dot(ws_ref[h], vn_[:, sl], preferred_element_type=F32) + bt_ref[:, h:h + 1]
        g_scr[:, sl] = gu[:, sl] * sv
    gg = g_scr[...]
    mix_ref[:, ATTN_W:] = (gg * _rms_scale(gg) * gg_ref[...]).astype(mix_ref.dtype)


def _mix(z3, bias, sink, q_gain, k_gain, attn_out_gain, v_gain, w_s, b_t, gmlp_out_gain):
    bsz, seq, _ = z3.shape
    nblk = seq // BLOCK
    assert nblk >= 2
    kcol = ATTN_W // KV_W
    vcol = kcol + 1
    ucol = (ATTN_W + 2 * KV_W) // 1024
    gcol = ucol + 2

    def zspec(width, col, shift=0):
        def imap(b, n):
            return (b, jnp.clip(n + shift, 0, nblk - 1), col)
        return pl.BlockSpec((None, BLOCK, width), imap)

    def full(shape):
        return pl.BlockSpec(shape, lambda b, n: (0,) * len(shape))

    in_specs = [
        pl.BlockSpec(memory_space=pltpu.SMEM),
        zspec(ATTN_W, 0),
        zspec(KV_W, kcol, -1), zspec(KV_W, kcol), zspec(KV_W, kcol, 1),
        zspec(KV_W, vcol, -1), zspec(KV_W, vcol), zspec(KV_W, vcol, 1),
        zspec(1024, ucol), zspec(1024, ucol + 1),
        zspec(1024, gcol), zspec(1024, gcol + 1),
        pl.BlockSpec(bias.shape, lambda b, n: (0, 0, 0, 0), pipeline_mode=pl.Buffered(1)),
        full((1, HEAD_DIM)), full((1, HEAD_DIM)), full((1, ATTN_W)),
        full((1, GMLP_W)), full(w_s.shape), full(b_t.shape), full((1, GMLP_W)),
    ]
    return pl.pallas_call(
        _mix_kernel,
        grid=(bsz, nblk),
        in_specs=in_specs,
        out_specs=pl.BlockSpec((None, BLOCK, ATTN_W + GMLP_W), lambda b, n: (b, n, 0)),
        out_shape=jax.ShapeDtypeStruct((bsz, seq, ATTN_W + GMLP_W), BF16),
        scratch_shapes=[pltpu.VMEM((BLOCK, ATTN_W), F32), pltpu.VMEM((BLOCK, GMLP_W), F32)],
        compiler_params=_params(("arbitrary", "arbitrary")),
        name="mix",
    )(sink, z3, z3, z3, z3, z3, z3, z3, z3, z3, z3, z3, bias,
      q_gain.reshape(1, HEAD_DIM), k_gain.reshape(1, HEAD_DIM), attn_out_gain.reshape(1, ATTN_W),
      v_gain.reshape(1, GMLP_W), w_s, b_t, gmlp_out_gain.reshape(1, GMLP_W))


def kernel(x, norm1, w_in, q_gain, k_gain, rel_bias, attn_sink, attn_out_gain,
           gmlp_v_gain, gmlp_w_s, gmlp_b_s, gmlp_out_gain, w_out, norm2, w1, w2):
    bsz, seq, d = x.shape
    depth = norm1.shape[0]
    bias = _bias_tables(rel_bias)
    x2d = x.reshape(bsz * seq, d)
    for l in range(depth):
        z = _inproj(x2d, norm1[l], w_in[l])
        mix = _mix(z.reshape(bsz, seq, -1), bias, attn_sink[l], q_gain[l], k_gain[l],
                   attn_out_gain[l], gmlp_v_gain[l], gmlp_w_s[l].astype(BF16),
                   gmlp_b_s[l].T, gmlp_out_gain[l])
        x1, x1g, inv2 = _outproj(mix.reshape(bsz * seq, -1), w_out[l], x2d, norm2[l])
        hid = _up(x1g, w1[l], inv2)
        x2d = _down(hid, w2[l], x1)
    return x2d.reshape(bsz, seq, d)
```

```python
import functools
import math

import jax
import jax.numpy as jnp
import numpy as np
from jax import lax
from jax.experimental import pallas as pl
from jax.experimental.pallas import tpu as pltpu

HEAD_DIM = 128
N_Q_HEADS = 16
N_KV_HEADS = 4
GQA_GROUP = N_Q_HEADS // N_KV_HEADS
ATTN_W = N_Q_HEADS * HEAD_DIM
KV_W = N_KV_HEADS * HEAD_DIM
WINDOW = 128
BLOCK = 128
N_BUCKETS = 32
MAX_DISTANCE = 128
GMLP_HEADS = 16
GMLP_W = GMLP_HEADS * HEAD_DIM
EPS = 1e-6
NEG = -1e30
LOG2E = math.log2(math.e)

F32 = jnp.float32
BF16 = jnp.bfloat16

V7X_VMEM_LIMIT_BYTES = 60 * 1024 * 1024


def _params(sem):
    return pltpu.CompilerParams(dimension_semantics=sem,
                                vmem_limit_bytes=V7X_VMEM_LIMIT_BYTES)


def _prep_kernel(x_ref, g_ref, xg_ref, inv_ref):
    x = x_ref[...]
    d = x.shape[-1]
    ss = jnp.sum(x * x, axis=-1, keepdims=True)
    inv_ref[...] = lax.rsqrt(ss * (1.0 / d) + EPS)
    xg_ref[...] = (x * g_ref[...]).astype(BF16)


def _prep(x2d, gain, n_rows, rows=256):
    m, d = n_rows, x2d.shape[1]
    return pl.pallas_call(
        _prep_kernel,
        grid=(m // rows,),
        in_specs=[pl.BlockSpec((rows, d), lambda i: (i, 0)),
                  pl.BlockSpec((1, d), lambda i: (0, 0))],
        out_specs=[pl.BlockSpec((rows, d), lambda i: (i, 0)),
                   pl.BlockSpec((rows, 1), lambda i: (i, 0))],
        out_shape=[jax.ShapeDtypeStruct((m, d), BF16),
                   jax.ShapeDtypeStruct((m, 1), F32)],
        compiler_params=_params(("arbitrary",)),
        name="prep_norm",
    )(x2d, gain.reshape(1, d))


FIRST_TN = 512


def _proj_kernel(a_ref, w_ref, *refs, body, n_in, n_prev, n_out, cast_w, scale_w, act_from):
    if scale_w:
        ws_ref, refs = refs[0], refs[1:]
    ins = refs[:n_in]
    outs = refs[n_in + n_prev:n_in + n_prev + n_out]
    rest = refs[n_in + n_prev + n_out:]
    if cast_w:
        wb_ref, rest = rest[0], rest[1:]

    def run(**body_kw):
        if cast_w:
            w = w_ref[...]
            if scale_w:
                w = w * ws_ref[...]
            w = w.astype(BF16)
            wb_ref[...] = w
        else:
            w = w_ref[...]
        acc = jnp.dot(a_ref[...], w, preferred_element_type=F32)
        body(acc, ins, outs, rest, **body_kw)

    if act_from is None:
        run()
    else:
        j = pl.program_id(1)
        pl.when(j < act_from)(functools.partial(run, act=False))
        pl.when(j >= act_from)(functools.partial(run, act=True))


def _project(body, name, a, w, ins, outs, *, tm, tn, prev=None, scratch=(), w_scale=None,
             act_col=None):
    m, k = a.shape
    n = w.shape[1]
    first = prev is None
    ioff = 0 if first else 1
    ni = 1 if first else m // tm - 1

    def spec(kind):
        if kind == "row":
            return pl.BlockSpec((tm, 1), lambda i, j: (i + ioff, 0))
        if kind == "col":
            return pl.BlockSpec((1, tn), lambda i, j: (0, j))
        return pl.BlockSpec((tm, tn), lambda i, j: (i + ioff, j))

    a_mode = dict(pipeline_mode=pl.Buffered(1)) if first else {}
    in_specs = [pl.BlockSpec((tm, k), lambda i, j: (i + ioff, 0), **a_mode),
                pl.BlockSpec((k, tn), lambda i, j: (0, j))]
    args = [a, w]
    scale_w = first and w_scale is not None
    if scale_w:
        in_specs.append(pl.BlockSpec((k, 1), lambda i, j: (0, 0), pipeline_mode=pl.Buffered(1)))
        args.append(w_scale)
    in_specs += [spec(kind) for _, kind in ins]
    args += [arr for arr, _ in ins]
    out_specs = [spec(kind) for _, kind in outs]
    out_shape = [sds for sds, _ in outs]
    aliases = {}
    if first:
        out_specs.append(pl.BlockSpec((k, tn), lambda i, j: (0, j)))
        out_shape.append(jax.ShapeDtypeStruct((k, n), BF16))
    else:
        for idx, p in enumerate(prev):
            aliases[len(args)] = idx
            in_specs.append(pl.BlockSpec(memory_space=pl.ANY))
            args.append(p)
    assert act_col is None or act_col % tn == 0
    return pl.pallas_call(
        functools.partial(_proj_kernel, body=body, n_in=len(ins), n_prev=len(aliases),
                          n_out=len(outs), cast_w=first, scale_w=scale_w,
                          act_from=None if act_col is None else act_col // tn),
        grid=(ni, n // tn),
        in_specs=in_specs,
        out_specs=out_specs,
        out_shape=out_shape,
        input_output_aliases=aliases,
        scratch_shapes=list(scratch),
        compiler_params=_params(("arbitrary", "arbitrary")),
        name=name + ("_first" if first else "_rest"),
    )(*args)


def _two_pass(body, name, a, w_f32, ins, outs, *, tm, tn, scratch=(), w_scale=None):
    *part, wb = _project(body, name, a, w_f32, ins, outs, tm=tm, tn=FIRST_TN, scratch=scratch,
                         w_scale=w_scale)
    return _project(body, name, a, wb, ins, outs, tm=tm, tn=tn, prev=part, scratch=scratch)


def _gelu_tanh(x):
    c = math.sqrt(2.0 / math.pi)
    inner = x * (c + (c * 0.044715) * (x * x))
    return (0.5 * x) * (1.0 + jnp.tanh(inner))


def _scale_body(acc, ins, outs, scratch, act=False):
    (inv_ref,), (o_ref,) = ins, outs
    t = acc * inv_ref[...]
    o_ref[...] = (_gelu_tanh(t) if act else t).astype(o_ref.dtype)


def _relu2_body(acc, ins, outs, scratch):
    (inv_ref,), (o_ref,) = ins, outs
    r = jnp.maximum(acc * inv_ref[...], 0.0)
    o_ref[...] = (r * r).astype(o_ref.dtype)


def _residual_norm_body(acc, ins, outs, scratch, *, d_model):
    (x_ref, g_ref), (x1_ref, x1g_ref, inv_ref), (ss_ref,) = ins, outs, scratch
    j = pl.program_id(1)
    x1 = x_ref[...] + acc
    x1_ref[...] = x1
    x1g_ref[...] = (x1 * g_ref[...]).astype(BF16)
    part = jnp.sum(x1 * x1, axis=-1, keepdims=True)

    @pl.when(j == 0)
    def _():
        ss_ref[...] = part

    @pl.when(j > 0)
    def _():
        ss_ref[...] += part

    @pl.when(j == pl.num_programs(1) - 1)
    def _():
        inv_ref[...] = lax.rsqrt(ss_ref[...] * (1.0 / d_model) + EPS)


PREP_ROWS = 256


def _inproj_rest_kernel(x_ref, g_ref, w_ref, z_prev_ref, o_ref, xg_a, xg_b, inv_a, inv_b,
                        *, n_row_blocks, prep_steps, act_from):
    del z_prev_ref
    r = pl.program_id(0)
    j = pl.program_id(1)

    def project(xg_in, inv_in, act):
        acc = jnp.dot(xg_in[...], w_ref[...], preferred_element_type=F32)
        t = acc * inv_in[...]
        o_ref[...] = (_gelu_tanh(t) if act else t).astype(o_ref.dtype)

    def prep(xg_out, inv_out):
        x = x_ref[...]
        rows = pl.ds(pl.multiple_of(j * PREP_ROWS, PREP_ROWS), PREP_ROWS)
        ss = jnp.sum(x * x, axis=-1, keepdims=True)
        inv_out[rows, :] = lax.rsqrt(ss * (1.0 / x.shape[-1]) + EPS)
        xg_out[rows, :] = (x * g_ref[...]).astype(BF16)

    has_next = r < n_row_blocks - 1
    prepping = has_next & (j < prep_steps)
    for parity, (bufs_in, bufs_out) in enumerate((((xg_a, inv_a), (xg_b, inv_b)),
                                                  ((xg_b, inv_b), (xg_a, inv_a)))):
        mine = lax.rem(r, 2) == parity

        @pl.when(mine & (r == 0) & prepping)
        def _():
            prep(*bufs_out)

        for act in (False, True):
            variant = mine & (r > 0) & ((j >= act_from) == act)

            @pl.when(variant & prepping)
            def _():
                project(*bufs_in, act)
                prep(*bufs_out)

            @pl.when(variant & jnp.logical_not(prepping))
            def _():
                project(*bufs_in, act)


def _inproj(x2d, gain, w_in, act_col, tm=1024, tn=1024):
    m, k = x2d.shape
    n = w_in.shape[1]
    ni, nj = m // tm, n // tn
    prep_steps = tm // PREP_ROWS
    assert prep_steps <= nj and act_col % tn == 0
    xg0, inv0 = _prep(x2d, gain, tm)
    z_part, wb = _project(_scale_body, "in_proj", xg0, w_in, [(inv0, "row")],
                          [(jax.ShapeDtypeStruct((m, n), BF16), "tile")], tm=tm, tn=FIRST_TN,
                          act_col=act_col)

    def x_map(r, j):
        return (jnp.minimum(r + 1, ni - 1) * prep_steps + jnp.minimum(j, prep_steps - 1), 0)

    def col(r, j):
        return jnp.where(r == 0, 0, j)

    return pl.pallas_call(
        functools.partial(_inproj_rest_kernel, n_row_blocks=ni, prep_steps=prep_steps,
                          act_from=act_col // tn),
        grid=(ni, nj),
        in_specs=[pl.BlockSpec((PREP_ROWS, k), x_map),
                  pl.BlockSpec((1, k), lambda r, j: (0, 0)),
                  pl.BlockSpec((k, tn), lambda r, j: (0, col(r, j))),
                  pl.BlockSpec(memory_space=pl.ANY)],
        out_specs=pl.BlockSpec((tm, tn), lambda r, j: (jnp.maximum(r, 1), col(r, j))),
        out_shape=jax.ShapeDtypeStruct((m, n), BF16),
        input_output_aliases={3: 0},
        scratch_shapes=[pltpu.VMEM((tm, k), BF16), pltpu.VMEM((tm, k), BF16),
                        pltpu.VMEM((tm, 1), F32), pltpu.VMEM((tm, 1), F32)],
        compiler_params=_params(("arbitrary", "arbitrary")),
        name="in_proj_rest",
    )(x2d, gain.reshape(1, k), wb, z_part)


def _up(x1g, w1, inv, tm=1024, tn=1024):
    m, n = x1g.shape[0], w1.shape[1]
    (hid,) = _two_pass(_relu2_body, "mlp_up", x1g, w1, [(inv, "row")],
                       [(jax.ShapeDtypeStruct((m, n), BF16), "tile")], tm=tm, tn=tn)
    return hid


def _outproj(mix, w_out, mix_gain, x2d, gain, tm=1024, tn=512):
    m, n = x2d.shape
    outs = [(jax.ShapeDtypeStruct((m, n), F32), "tile"),
            (jax.ShapeDtypeStruct((m, n), BF16), "tile"),
            (jax.ShapeDtypeStruct((m, 1), F32), "row")]
    return _two_pass(functools.partial(_residual_norm_body, d_model=n), "out_proj", mix, w_out,
                     [(x2d, "tile"), (gain.reshape(1, n), "col")], outs, tm=tm, tn=tn,
                     scratch=[pltpu.VMEM((tm, 1), F32)], w_scale=mix_gain.reshape(-1, 1))


def _down_kernel(a_ref, w_ref, x1_ref, *refs, cast_w):
    o_ref = refs[0] if cast_w else refs[1]

    @pl.when(pl.program_id(2) == 0)
    def _():
        o_ref[...] = x1_ref[...]

    if cast_w:
        wb_ref = refs[1]
        w = w_ref[...].astype(BF16)
        wb_ref[...] = w
    else:
        w = w_ref[...]
    o_ref[...] += jnp.dot(a_ref[...], w, preferred_element_type=F32)


def _down_call(a, w, x1, *, tm, tn, tk, prev=None):
    m, k = a.shape
    n = w.shape[1]
    first = prev is None
    ioff = 0 if first else 1
    ni = 1 if first else m // tm - 1
    in_specs = [pl.BlockSpec((tm, tk), lambda i, j, kk: (i + ioff, kk)),
                pl.BlockSpec((tk, tn), lambda i, j, kk: (kk, j)),
                pl.BlockSpec((tm, tn), lambda i, j, kk: (i + ioff, j))]
    args = [a, w, x1]
    out_specs = [pl.BlockSpec((tm, tn), lambda i, j, kk: (i + ioff, j))]
    out_shape = [jax.ShapeDtypeStruct((m, n), F32)]
    aliases = {}
    if first:
        out_specs.append(pl.BlockSpec((tk, tn), lambda i, j, kk: (kk, j)))
        out_shape.append(jax.ShapeDtypeStruct((k, n), BF16))
    else:
        aliases[len(args)] = 0
        in_specs.append(pl.BlockSpec(memory_space=pl.ANY))
        args.append(prev)
    return pl.pallas_call(
        functools.partial(_down_kernel, cast_w=first),
        grid=(ni, n // tn, k // tk),
        in_specs=in_specs,
        out_specs=out_specs,
        out_shape=out_shape,
        input_output_aliases=aliases,
        compiler_params=_params(("arbitrary", "arbitrary", "arbitrary")),
        name="mlp_down" + ("_first" if first else "_rest"),
    )(*args)


def _down(hid, w2, x1, tm=1024, tn=1024, tk=4096):
    part, wb = _down_call(hid, w2, x1, tm=tm, tn=tn, tk=tk // 2)
    (y,) = _down_call(hid, wb, x1, tm=tm, tn=tn, tk=tk, prev=part)
    return y


def _t5_bucket(rel):
    nb = N_BUCKETS // 2
    max_exact = nb // 2
    ret = (rel > 0).astype(np.int32) * nb
    n = np.abs(rel)
    large = max_exact + (np.log(np.maximum(n, 1).astype(np.float32) / max_exact)
                         / math.log(MAX_DISTANCE / max_exact) * (nb - max_exact)).astype(np.int32)
    large = np.minimum(large, nb - 1)
    return ret + np.where(n < max_exact, n, large)


def _bucket_table():
    a = np.arange(BLOCK)[:, None]
    s = np.arange(3 * BLOCK)[None, :]
    rel = s - BLOCK - a
    return np.where(np.abs(rel) <= WINDOW, _t5_bucket(rel), -1).astype(np.int32)


def _bias_kernel(rb_ref, bk_ref, o_ref):
    h = pl.program_id(0)
    bk = bk_ref[...]
    acc = jnp.full(bk.shape, NEG, F32)
    for b in range(N_BUCKETS):
        acc = jnp.where(bk == b, rb_ref[b, h] * LOG2E, acc)
    col = lax.broadcasted_iota(jnp.int32, bk.shape, 1)
    o_ref[0] = jnp.where(col >= BLOCK, acc, NEG)
    o_ref[1] = acc
    o_ref[2] = jnp.where(col < 2 * BLOCK, acc, NEG)


def _bias_tables(rel_bias):
    return pl.pallas_call(
        _bias_kernel,
        grid=(N_Q_HEADS,),
        in_specs=[pl.BlockSpec(memory_space=pltpu.SMEM),
                  pl.BlockSpec((BLOCK, 3 * BLOCK), lambda h: (0, 0))],
        out_specs=pl.BlockSpec((3, None, BLOCK, 3 * BLOCK), lambda h: (0, h, 0, 0)),
        out_shape=jax.ShapeDtypeStruct((3, N_Q_HEADS, BLOCK, 3 * BLOCK), F32),
        compiler_params=_params(("arbitrary",)),
        name="bias_tables",
    )(rel_bias, jnp.asarray(_bucket_table()))


def _rms_scale(t):
    return lax.rsqrt(jnp.sum(t * t, axis=-1, keepdims=True) * (1.0 / t.shape[-1]) + EPS)


def _mix_kernel(sink_ref, q_ref, kp_ref, ko_ref, kn_ref, vp_ref, vo_ref, vn_ref,
                u0_ref, u1_ref, g0_ref, g1_ref, bias_ref, qg_ref, kg_ref,
                vg_ref, ws_ref, bt_ref, mix_ref, a_scr, g_scr):
    n = pl.program_id(1)
    nlast = pl.num_programs(1) - 1
    variant = jnp.where(n == 0, 0, jnp.where(n == nlast, 2, 1))
    kfold = kg_ref[...] * qg_ref[...] * (HEAD_DIM ** -0.5 * LOG2E)

    kband = jnp.concatenate([kp_ref[...], ko_ref[...], kn_ref[...]], axis=0).astype(F32)
    vband = jnp.concatenate([vp_ref[...], vo_ref[...], vn_ref[...]], axis=0)
    ones = jnp.ones((3 * BLOCK, HEAD_DIM), BF16)
    for kh in range(N_KV_HEADS):
        k = kband[:, kh * HEAD_DIM:(kh + 1) * HEAD_DIM]
        kn = (k * _rms_scale(k) * kfold).astype(BF16)
        v1 = jnp.concatenate([vband[:, kh * HEAD_DIM:(kh + 1) * HEAD_DIM], ones], axis=1)
        qs = []
        for g in range(GQA_GROUP):
            h = kh * GQA_GROUP + g
            qh = q_ref[:, h * HEAD_DIM:(h + 1) * HEAD_DIM].astype(F32)
            qs.append((qh * _rms_scale(qh)).astype(BF16))
        qstack = jnp.concatenate(qs, axis=0)
        s = lax.dot_general(qstack, kn, (((1,), (1,)), ((), ())),
                            preferred_element_type=F32)
        ps, sink_terms = [], []
        for g in range(GQA_GROUP):
            h = kh * GQA_GROUP + g
            sink = sink_ref[h] * LOG2E
            sg = s[g * BLOCK:(g + 1) * BLOCK] + bias_ref[variant, h]
            m = jnp.maximum(jnp.max(sg, axis=-1, keepdims=True), sink)
            ps.append(jnp.exp2(sg - m).astype(BF16))
            sink_terms.append(jnp.exp2(sink - m))
        pstack = jnp.concatenate(ps, axis=0)
        ol = jnp.dot(pstack, v1, preferred_element_type=F32)
        for g in range(GQA_GROUP):
            h = kh * GQA_GROUP + g
            rows = slice(g * BLOCK, (g + 1) * BLOCK)
            denom = ol[rows, HEAD_DIM:] + sink_terms[g]
            a_scr[:, h * HEAD_DIM:(h + 1) * HEAD_DIM] = ol[rows, :HEAD_DIM] * (1.0 / denom)

    a = a_scr[...]
    mix_ref[:, :ATTN_W] = (a * _rms_scale(a)).astype(mix_ref.dtype)

    gu = jnp.concatenate([u0_ref[...], u1_ref[...]], axis=1).astype(F32)
    gv = jnp.concatenate([g0_ref[...], g1_ref[...]], axis=1).astype(F32)
    vn_ = (gv * _rms_scale(gv) * vg_ref[...]).astype(BF16)
    for h in range(GMLP_HEADS):
        sl = slice(h * HEAD_DIM, (h + 1) * HEAD_DIM)
        sv = jnp.dot(ws_ref[h], vn_[:, sl], preferred_element_type=F32) + bt_ref[:, h:h + 1]
        g_scr[:, sl] = gu[:, sl] * sv
    gg = g_scr[...]
    mix_ref[:, ATTN_W:] = (gg * _rms_scale(gg)).astype(mix_ref.dtype)


def _mix(z3, bias, sink, q_gain, k_gain, v_gain, w_s, b_t):
    bsz, seq, _ = z3.shape
    nblk = seq // BLOCK
    assert nblk >= 2
    kcol = ATTN_W // KV_W
    vcol = kcol + 1
    ucol = (ATTN_W + 2 * KV_W) // 1024
    gcol = ucol + 2

    def zspec(width, col, shift=0):
        def imap(b, n):
            return (b, jnp.clip(n + shift, 0, nblk - 1), col)
        return pl.BlockSpec((None, BLOCK, width), imap)

    def full(shape):
        return pl.BlockSpec(shape, lambda b, n: (0,) * len(shape))

    in_specs = [
        pl.BlockSpec(memory_space=pltpu.SMEM),
        zspec(ATTN_W, 0),
        zspec(KV_W, kcol, -1), zspec(KV_W, kcol), zspec(KV_W, kcol, 1),
        zspec(KV_W, vcol, -1), zspec(KV_W, vcol), zspec(KV_W, vcol, 1),
        zspec(1024, ucol), zspec(1024, ucol + 1),
        zspec(1024, gcol), zspec(1024, gcol + 1),
        pl.BlockSpec(bias.shape, lambda b, n: (0, 0, 0, 0), pipeline_mode=pl.Buffered(1)),
        full((1, HEAD_DIM)), full((1, HEAD_DIM)),
        full((1, GMLP_W)), full(w_s.shape), full(b_t.shape),
    ]
    return pl.pallas_call(
        _mix_kernel,
        grid=(bsz, nblk),
        in_specs=in_specs,
        out_specs=pl.BlockSpec((None, BLOCK, ATTN_W + GMLP_W), lambda b, n: (b, n, 0)),
        out_shape=jax.ShapeDtypeStruct((bsz, seq, ATTN_W + GMLP_W), BF16),
        scratch_shapes=[pltpu.VMEM((BLOCK, ATTN_W), F32), pltpu.VMEM((BLOCK, GMLP_W), F32)],
        compiler_params=_params(("arbitrary", "arbitrary")),
        name="mix",
    )(sink, z3, z3, z3, z3, z3, z3, z3, z3, z3, z3, z3, bias,
      q_gain.reshape(1, HEAD_DIM), k_gain.reshape(1, HEAD_DIM),
      v_gain.reshape(1, GMLP_W), w_s, b_t)


def kernel(x, norm1, w_in, q_gain, k_gain, rel_bias, attn_sink, attn_out_gain,
           gmlp_v_gain, gmlp_w_s, gmlp_b_s, gmlp_out_gain, w_out, norm2, w1, w2):
    bsz, seq, d = x.shape
    depth = norm1.shape[0]
    bias = _bias_tables(rel_bias)
    x2d = x.reshape(bsz * seq, d)
    for l in range(depth):
        z = _inproj(x2d, norm1[l], w_in[l], act_col=ATTN_W + 2 * KV_W)
        mix = _mix(z.reshape(bsz, seq, -1), bias, attn_sink[l], q_gain[l], k_gain[l],
                   gmlp_v_gain[l], gmlp_w_s[l].astype(BF16), gmlp_b_s[l].T)
        mix_gain = jnp.concatenate([attn_out_gain[l], gmlp_out_gain[l]])
        x1, x1g, inv2 = _outproj(mix.reshape(bsz * seq, -1), w_out[l], mix_gain, x2d, norm2[l])
        hid = _up(x1g, w1[l], inv2)
        x2d = _down(hid, w2[l], x1)
    return x2d.reshape(bsz, seq, d)
```

```python
import functools
import math

import jax
import jax.numpy as jnp
import numpy as np
from jax import lax
from jax.experimental import pallas as pl
from jax.experimental.pallas import tpu as pltpu

HEAD_DIM = 128
N_Q_HEADS = 16
N_KV_HEADS = 4
GQA_GROUP = N_Q_HEADS // N_KV_HEADS
ATTN_W = N_Q_HEADS * HEAD_DIM
KV_W = N_KV_HEADS * HEAD_DIM
WINDOW = 128
BLOCK = 128
N_BUCKETS = 32
MAX_DISTANCE = 128
GMLP_HEADS = 16
GMLP_W = GMLP_HEADS * HEAD_DIM
EPS = 1e-6
NEG = -1e30
LOG2E = math.log2(math.e)

F32 = jnp.float32
BF16 = jnp.bfloat16

V7X_VMEM_LIMIT_BYTES = 60 * 1024 * 1024


def _params(sem):
    return pltpu.CompilerParams(dimension_semantics=sem,
                                vmem_limit_bytes=V7X_VMEM_LIMIT_BYTES)


def _prep_kernel(x_ref, g_ref, xg_ref, inv_ref):
    x = x_ref[...]
    d = x.shape[-1]
    ss = jnp.sum(x * x, axis=-1, keepdims=True)
    inv_ref[...] = lax.rsqrt(ss * (1.0 / d) + EPS)
    xg_ref[...] = (x * g_ref[...]).astype(BF16)


def _prep(x2d, gain, n_rows, rows=256):
    m, d = n_rows, x2d.shape[1]
    return pl.pallas_call(
        _prep_kernel,
        grid=(m // rows,),
        in_specs=[pl.BlockSpec((rows, d), lambda i: (i, 0)),
                  pl.BlockSpec((1, d), lambda i: (0, 0))],
        out_specs=[pl.BlockSpec((rows, d), lambda i: (i, 0)),
                   pl.BlockSpec((rows, 1), lambda i: (i, 0))],
        out_shape=[jax.ShapeDtypeStruct((m, d), BF16),
                   jax.ShapeDtypeStruct((m, 1), F32)],
        compiler_params=_params(("arbitrary",)),
        name="prep_norm",
    )(x2d, gain.reshape(1, d))


FIRST_TN = 512


def _proj_kernel(a_ref, w_ref, *refs, body, n_in, n_prev, n_out, cast_w, scale_w, act_from):
    if scale_w:
        ws_ref, refs = refs[0], refs[1:]
    ins = refs[:n_in]
    outs = refs[n_in + n_prev:n_in + n_prev + n_out]
    rest = refs[n_in + n_prev + n_out:]
    if cast_w:
        wb_ref, rest = rest[0], rest[1:]

    def run(**body_kw):
        if cast_w:
            w = w_ref[...]
            if scale_w:
                w = w * ws_ref[...]
            w = w.astype(BF16)
            wb_ref[...] = w
        else:
            w = w_ref[...]
        acc = jnp.dot(a_ref[...], w, preferred_element_type=F32)
        body(acc, ins, outs, rest, **body_kw)

    if act_from is None:
        run()
    else:
        j = pl.program_id(1)
        pl.when(j < act_from)(functools.partial(run, act=False))
        pl.when(j >= act_from)(functools.partial(run, act=True))


def _project(body, name, a, w, ins, outs, *, tm, tn, prev=None, scratch=(), w_scale=None,
             act_col=None):
    m, k = a.shape
    n = w.shape[1]
    first = prev is None
    ioff = 0 if first else 1
    ni = 1 if first else m // tm - 1

    def spec(kind):
        if kind == "row":
            return pl.BlockSpec((tm, 1), lambda i, j: (i + ioff, 0))
        if kind == "col":
            return pl.BlockSpec((1, tn), lambda i, j: (0, j))
        return pl.BlockSpec((tm, tn), lambda i, j: (i + ioff, j))

    a_mode = dict(pipeline_mode=pl.Buffered(1)) if first else {}
    in_specs = [pl.BlockSpec((tm, k), lambda i, j: (i + ioff, 0), **a_mode),
                pl.BlockSpec((k, tn), lambda i, j: (0, j))]
    args = [a, w]
    scale_w = first and w_scale is not None
    if scale_w:
        in_specs.append(pl.BlockSpec((k, 1), lambda i, j: (0, 0), pipeline_mode=pl.Buffered(1)))
        args.append(w_scale)
    in_specs += [spec(kind) for _, kind in ins]
    args += [arr for arr, _ in ins]
    out_specs = [spec(kind) for _, kind in outs]
    out_shape = [sds for sds, _ in outs]
    aliases = {}
    if first:
        out_specs.append(pl.BlockSpec((k, tn), lambda i, j: (0, j)))
        out_shape.append(jax.ShapeDtypeStruct((k, n), BF16))
    else:
        for idx, p in enumerate(prev):
            aliases[len(args)] = idx
            in_specs.append(pl.BlockSpec(memory_space=pl.ANY))
            args.append(p)
    assert act_col is None or act_col % tn == 0
    return pl.pallas_call(
        functools.partial(_proj_kernel, body=body, n_in=len(ins), n_prev=len(aliases),
                          n_out=len(outs), cast_w=first, scale_w=scale_w,
                          act_from=None if act_col is None else act_col // tn),
        grid=(ni, n // tn),
        in_specs=in_specs,
        out_specs=out_specs,
        out_shape=out_shape,
        input_output_aliases=aliases,
        scratch_shapes=list(scratch),
        compiler_params=_params(("arbitrary", "arbitrary")),
        name=name + ("_first" if first else "_rest"),
    )(*args)


def _two_pass(body, name, a, w_f32, ins, outs, *, tm, tn, scratch=(), w_scale=None):
    *part, wb = _project(body, name, a, w_f32, ins, outs, tm=tm, tn=FIRST_TN, scratch=scratch,
                         w_scale=w_scale)
    return _project(body, name, a, wb, ins, outs, tm=tm, tn=tn, prev=part, scratch=scratch)


def _gelu_tanh(x):
    c = math.sqrt(2.0 / math.pi)
    inner = x * (c + (c * 0.044715) * (x * x))
    return (0.5 * x) * (1.0 + jnp.tanh(inner))


def _scale_body(acc, ins, outs, scratch, act=False):
    (inv_ref,), (o_ref,) = ins, outs
    t = acc * inv_ref[...]
    o_ref[...] = (_gelu_tanh(t) if act else t).astype(o_ref.dtype)


def _relu2_body(acc, ins, outs, scratch):
    (inv_ref,), (o_ref,) = ins, outs
    r = jnp.maximum(acc * inv_ref[...], 0.0)
    o_ref[...] = (r * r).astype(o_ref.dtype)


def _residual_norm_body(acc, ins, outs, scratch, *, d_model):
    (x_ref, g_ref), (x1_ref, x1g_ref, inv_ref), (ss_ref,) = ins, outs, scratch
    j = pl.program_id(1)
    x1 = x_ref[...] + acc
    x1_ref[...] = x1
    x1g_ref[...] = (x1 * g_ref[...]).astype(BF16)
    part = jnp.sum(x1 * x1, axis=-1, keepdims=True)

    @pl.when(j == 0)
    def _():
        ss_ref[...] = part

    @pl.when(j > 0)
    def _():
        ss_ref[...] += part

    @pl.when(j == pl.num_programs(1) - 1)
    def _():
        inv_ref[...] = lax.rsqrt(ss_ref[...] * (1.0 / d_model) + EPS)


PREP_ROWS = 256


def _inproj_rest_kernel(x_ref, g_ref, w_ref, z_prev_ref, o_ref, xg_scr, inv_scr,
                        *, n_row_blocks, prep_steps, act_from):
    del z_prev_ref
    r = pl.program_id(0)
    j = pl.program_id(1)
    slot_in, slot_out = lax.rem(r, 2), lax.rem(r + 1, 2)

    def project(act):
        acc = jnp.dot(xg_scr[slot_in], w_ref[...], preferred_element_type=F32)
        t = acc * inv_scr[slot_in]
        o_ref[...] = (_gelu_tanh(t) if act else t).astype(o_ref.dtype)

    def prep():
        x = x_ref[...]
        rows = pl.ds(pl.multiple_of(j * PREP_ROWS, PREP_ROWS), PREP_ROWS)
        ss = jnp.sum(x * x, axis=-1, keepdims=True)
        inv_scr[slot_out, rows, :] = lax.rsqrt(ss * (1.0 / x.shape[-1]) + EPS)
        xg_scr[slot_out, rows, :] = (x * g_ref[...]).astype(BF16)

    has_next = r < n_row_blocks - 1
    prepping = has_next & (j < prep_steps)

    @pl.when((r == 0) & prepping)
    def _():
        prep()

    for act in (False, True):
        variant = (r > 0) & ((j >= act_from) == act)

        @pl.when(variant & prepping)
        def _():
            project(act)
            prep()

        @pl.when(variant & jnp.logical_not(prepping))
        def _():
            project(act)


def _inproj(x2d, gain, w_in, act_col, tm=1024, tn=1024):
    m, k = x2d.shape
    n = w_in.shape[1]
    ni, nj = m // tm, n // tn
    prep_steps = tm // PREP_ROWS
    assert prep_steps <= nj and act_col % tn == 0
    xg0, inv0 = _prep(x2d, gain, tm)
    z_part, wb = _project(_scale_body, "in_proj", xg0, w_in, [(inv0, "row")],
                          [(jax.ShapeDtypeStruct((m, n), BF16), "tile")], tm=tm, tn=FIRST_TN,
                          act_col=act_col)

    def x_map(r, j):
        return (jnp.minimum(r + 1, ni - 1) * prep_steps + jnp.minimum(j, prep_steps - 1), 0)

    def col(r, j):
        return jnp.where(r == 0, 0, j)

    return pl.pallas_call(
        functools.partial(_inproj_rest_kernel, n_row_blocks=ni, prep_steps=prep_steps,
                          act_from=act_col // tn),
        grid=(ni, nj),
        in_specs=[pl.BlockSpec((PREP_ROWS, k), x_map),
                  pl.BlockSpec((1, k), lambda r, j: (0, 0)),
                  pl.BlockSpec((k, tn), lambda r, j: (0, col(r, j))),
                  pl.BlockSpec(memory_space=pl.ANY)],
        out_specs=pl.BlockSpec((tm, tn), lambda r, j: (jnp.maximum(r, 1), col(r, j))),
        out_shape=jax.ShapeDtypeStruct((m, n), BF16),
        input_output_aliases={3: 0},
        scratch_shapes=[pltpu.VMEM((2, tm, k), BF16), pltpu.VMEM((2, tm, 1), F32)],
        compiler_params=_params(("arbitrary", "arbitrary")),
        name="in_proj_rest",
    )(x2d, gain.reshape(1, k), wb, z_part)


def _up(x1g, w1, inv, tm=1024, tn=1024):
    m, n = x1g.shape[0], w1.shape[1]
    (hid,) = _two_pass(_relu2_body, "mlp_up", x1g, w1, [(inv, "row")],
                       [(jax.ShapeDtypeStruct((m, n), BF16), "tile")], tm=tm, tn=tn)
    return hid


def _outproj(mix, w_out, mix_gain, x2d, gain, tm=1024, tn=512):
    m, n = x2d.shape
    outs = [(jax.ShapeDtypeStruct((m, n), F32), "tile"),
            (jax.ShapeDtypeStruct((m, n), BF16), "tile"),
            (jax.ShapeDtypeStruct((m, 1), F32), "row")]
    return _two_pass(functools.partial(_residual_norm_body, d_model=n), "out_proj", mix, w_out,
                     [(x2d, "tile"), (gain.reshape(1, n), "col")], outs, tm=tm, tn=tn,
                     scratch=[pltpu.VMEM((tm, 1), F32)], w_scale=mix_gain.reshape(-1, 1))


def _down_kernel(a_ref, w_ref, x1_ref, *refs, cast_w):
    o_ref = refs[0] if cast_w else refs[1]

    @pl.when(pl.program_id(2) == 0)
    def _():
        o_ref[...] = x1_ref[...]

    if cast_w:
        wb_ref = refs[1]
        w = w_ref[...].astype(BF16)
        wb_ref[...] = w
    else:
        w = w_ref[...]
    o_ref[...] += jnp.dot(a_ref[...], w, preferred_element_type=F32)


def _down_call(a, w, x1, *, tm, tn, tk, prev=None):
    m, k = a.shape
    n = w.shape[1]
    first = prev is None
    ioff = 0 if first else 1
    ni = 1 if first else m // tm - 1
    in_specs = [pl.BlockSpec((tm, tk), lambda i, j, kk: (i + ioff, kk)),
                pl.BlockSpec((tk, tn), lambda i, j, kk: (kk, j)),
                pl.BlockSpec((tm, tn), lambda i, j, kk: (i + ioff, j))]
    args = [a, w, x1]
    out_specs = [pl.BlockSpec((tm, tn), lambda i, j, kk: (i + ioff, j))]
    out_shape = [jax.ShapeDtypeStruct((m, n), F32)]
    aliases = {}
    if first:
        out_specs.append(pl.BlockSpec((tk, tn), lambda i, j, kk: (kk, j)))
        out_shape.append(jax.ShapeDtypeStruct((k, n), BF16))
    else:
        aliases[len(args)] = 0
        in_specs.append(pl.BlockSpec(memory_space=pl.ANY))
        args.append(prev)
    return pl.pallas_call(
        functools.partial(_down_kernel, cast_w=first),
        grid=(ni, n // tn, k // tk),
        in_specs=in_specs,
        out_specs=out_specs,
        out_shape=out_shape,
        input_output_aliases=aliases,
        compiler_params=_params(("arbitrary", "arbitrary", "arbitrary")),
        name="mlp_down" + ("_first" if first else "_rest"),
    )(*args)


def _down(hid, w2, x1, tm=1024, tn=1024, tk=4096):
    part, wb = _down_call(hid, w2, x1, tm=tm, tn=tn, tk=tk // 2)
    (y,) = _down_call(hid, wb, x1, tm=tm, tn=tn, tk=tk, prev=part)
    return y


def _t5_bucket(rel):
    nb = N_BUCKETS // 2
    max_exact = nb // 2
    ret = (rel > 0).astype(np.int32) * nb
    n = np.abs(rel)
    large = max_exact + (np.log(np.maximum(n, 1).astype(np.float32) / max_exact)
                         / math.log(MAX_DISTANCE / max_exact) * (nb - max_exact)).astype(np.int32)
    large = np.minimum(large, nb - 1)
    return ret + np.where(n < max_exact, n, large)


def _bucket_table():
    a = np.arange(BLOCK)[:, None]
    s = np.arange(3 * BLOCK)[None, :]
    rel = s - BLOCK - a
    return np.where(np.abs(rel) <= WINDOW, _t5_bucket(rel), -1).astype(np.int32)


def _bias_kernel(rb_ref, bk_ref, o_ref):
    h = pl.program_id(0)
    bk = bk_ref[...]
    acc = jnp.full(bk.shape, NEG, F32)
    for b in range(N_BUCKETS):
        acc = jnp.where(bk == b, rb_ref[b, h] * LOG2E, acc)
    col = lax.broadcasted_iota(jnp.int32, bk.shape, 1)
    o_ref[0] = jnp.where(col >= BLOCK, acc, NEG)
    o_ref[1] = acc
    o_ref[2] = jnp.where(col < 2 * BLOCK, acc, NEG)


def _bias_tables(rel_bias):
    return pl.pallas_call(
        _bias_kernel,
        grid=(N_Q_HEADS,),
        in_specs=[pl.BlockSpec(memory_space=pltpu.SMEM),
                  pl.BlockSpec((BLOCK, 3 * BLOCK), lambda h: (0, 0))],
        out_specs=pl.BlockSpec((3, None, BLOCK, 3 * BLOCK), lambda h: (0, h, 0, 0)),
        out_shape=jax.ShapeDtypeStruct((3, N_Q_HEADS, BLOCK, 3 * BLOCK), F32),
        compiler_params=_params(("arbitrary",)),
        name="bias_tables",
    )(rel_bias, jnp.asarray(_bucket_table()))


def _rms_scale(t):
    return lax.rsqrt(jnp.sum(t * t, axis=-1, keepdims=True) * (1.0 / t.shape[-1]) + EPS)


MIX_BLOCKS = 2


def _mix_kernel(sink_ref, q_ref, kp_ref, ko_ref, kn_ref, vp_ref, vo_ref, vn_ref,
                u0_ref, u1_ref, g0_ref, g1_ref, bias_ref, qg_ref, kg_ref,
                vg_ref, ws_ref, bt_ref, mix_ref, a_scr, g_scr):
    n = pl.program_id(1)
    nlast = pl.num_programs(1) - 1
    kfold = kg_ref[...] * qg_ref[...] * (HEAD_DIM ** -0.5 * LOG2E)

    kband = jnp.concatenate([kp_ref[...], ko_ref[...], kn_ref[...]], axis=0).astype(F32)
    vband = jnp.concatenate([vp_ref[...], vo_ref[...], vn_ref[...]], axis=0)
    ones = jnp.ones((3 * BLOCK, HEAD_DIM), BF16)
    for kh in range(N_KV_HEADS):
        k = kband[:, kh * HEAD_DIM:(kh + 1) * HEAD_DIM]
        kn_all = (k * _rms_scale(k) * kfold).astype(BF16)
        for sb in range(MIX_BLOCKS):
            tok = slice(sb * BLOCK, (sb + 1) * BLOCK)
            band = slice(sb * BLOCK, (sb + 3) * BLOCK)
            if sb == 0:
                variant = jnp.where(n == 0, 0, 1)
            elif sb == MIX_BLOCKS - 1:
                variant = jnp.where(n == nlast, 2, 1)
            else:
                variant = 1
            kn = kn_all[band]
            v1 = jnp.concatenate([vband[band, kh * HEAD_DIM:(kh + 1) * HEAD_DIM], ones], axis=1)
            qs = []
            for g in range(GQA_GROUP):
                h = kh * GQA_GROUP + g
                qh = q_ref[tok, h * HEAD_DIM:(h + 1) * HEAD_DIM].astype(F32)
                qs.append((qh * _rms_scale(qh)).astype(BF16))
            qstack = jnp.concatenate(qs, axis=0)
            s = lax.dot_general(qstack, kn, (((1,), (1,)), ((), ())),
                                preferred_element_type=F32)
            ps, sink_terms = [], []
            for g in range(GQA_GROUP):
                h = kh * GQA_GROUP + g
                sink = sink_ref[h] * LOG2E
                sg = s[g * BLOCK:(g + 1) * BLOCK] + bias_ref[variant, h]
                m = jnp.maximum(jnp.max(sg, axis=-1, keepdims=True), sink)
                ps.append(jnp.exp2(sg - m).astype(BF16))
                sink_terms.append(jnp.exp2(sink - m))
            pstack = jnp.concatenate(ps, axis=0)
            ol = jnp.dot(pstack, v1, preferred_element_type=F32)
            for g in range(GQA_GROUP):
                h = kh * GQA_GROUP + g
                rows = slice(g * BLOCK, (g + 1) * BLOCK)
                denom = ol[rows, HEAD_DIM:] + sink_terms[g]
                a_scr[tok, h * HEAD_DIM:(h + 1) * HEAD_DIM] = ol[rows, :HEAD_DIM] * (1.0 / denom)

    a = a_scr[...]
    mix_ref[:, :ATTN_W] = (a * _rms_scale(a)).astype(mix_ref.dtype)

    gu = jnp.concatenate([u0_ref[...], u1_ref[...]], axis=1).astype(F32)
    gv = jnp.concatenate([g0_ref[...], g1_ref[...]], axis=1).astype(F32)
    vn_ = (gv * _rms_scale(gv) * vg_ref[...]).astype(BF16)
    for sb in range(MIX_BLOCKS):
        tok = slice(sb * BLOCK, (sb + 1) * BLOCK)
        for h in range(GMLP_HEADS):
            sl = slice(h * HEAD_DIM, (h + 1) * HEAD_DIM)
            sv = jnp.dot(ws_ref[h], vn_[tok, sl], preferred_element_type=F32) + bt_ref[:, h:h + 1]
            g_scr[tok, sl] = gu[tok, sl] * sv
    gg = g_scr[...]
    mix_ref[:, ATTN_W:] = (gg * _rms_scale(gg)).astype(mix_ref.dtype)


def _mix(z3, bias, sink, q_gain, k_gain, v_gain, w_s, b_t):
    bsz, seq, _ = z3.shape
    nblk = seq // BLOCK
    rows = MIX_BLOCKS * BLOCK
    assert nblk >= 2 and nblk % MIX_BLOCKS == 0
    kcol = ATTN_W // KV_W
    vcol = kcol + 1
    ucol = (ATTN_W + 2 * KV_W) // 1024
    gcol = ucol + 2

    def own(width, col):
        return pl.BlockSpec((None, rows, width), lambda b, n: (b, n, col))

    def edge(width, col, after):
        def imap(b, n):
            blk = n * MIX_BLOCKS + (MIX_BLOCKS if after else -1)
            return (b, jnp.clip(blk, 0, nblk - 1), col)
        return pl.BlockSpec((None, BLOCK, width), imap)

    def full(shape):
        return pl.BlockSpec(shape, lambda b, n: (0,) * len(shape))

    in_specs = [
        pl.BlockSpec(memory_space=pltpu.SMEM),
        own(ATTN_W, 0),
        edge(KV_W, kcol, False), own(KV_W, kcol), edge(KV_W, kcol, True),
        edge(KV_W, vcol, False), own(KV_W, vcol), edge(KV_W, vcol, True),
        own(1024, ucol), own(1024, ucol + 1),
        own(1024, gcol), own(1024, gcol + 1),
        pl.BlockSpec(bias.shape, lambda b, n: (0, 0, 0, 0), pipeline_mode=pl.Buffered(1)),
        full((1, HEAD_DIM)), full((1, HEAD_DIM)),
        full((1, GMLP_W)), full(w_s.shape), full(b_t.shape),
    ]
    return pl.pallas_call(
        _mix_kernel,
        grid=(bsz, nblk // MIX_BLOCKS),
        in_specs=in_specs,
        out_specs=pl.BlockSpec((None, rows, ATTN_W + GMLP_W), lambda b, n: (b, n, 0)),
        out_shape=jax.ShapeDtypeStruct((bsz, seq, ATTN_W + GMLP_W), BF16),
        scratch_shapes=[pltpu.VMEM((rows, ATTN_W), F32), pltpu.VMEM((rows, GMLP_W), F32)],
        compiler_params=_params(("arbitrary", "arbitrary")),
        name="mix",
    )(sink, z3, z3, z3, z3, z3, z3, z3, z3, z3, z3, z3, bias,
      q_gain.reshape(1, HEAD_DIM), k_gain.reshape(1, HEAD_DIM),
      v_gain.reshape(1, GMLP_W), w_s, b_t)


def kernel(x, norm1, w_in, q_gain, k_gain, rel_bias, attn_sink, attn_out_gain,
           gmlp_v_gain, gmlp_w_s, gmlp_b_s, gmlp_out_gain, w_out, norm2, w1, w2):
    bsz, seq, d = x.shape
    depth = norm1.shape[0]
    bias = _bias_tables(rel_bias)
    x2d = x.reshape(bsz * seq, d)
    for l in range(depth):
        z = _inproj(x2d, norm1[l], w_in[l], act_col=ATTN_W + 2 * KV_W)
        mix = _mix(z.reshape(bsz, seq, -1), bias, attn_sink[l], q_gain[l], k_gain[l],
                   gmlp_v_gain[l], gmlp_w_s[l].astype(BF16), gmlp_b_s[l].T)
        mix_gain = jnp.concatenate([attn_out_gain[l], gmlp_out_gain[l]])
        x1, x1g, inv2 = _outproj(mix.reshape(bsz * seq, -1), w_out[l], mix_gain, x2d, norm2[l])
        hid = _up(x1g, w1[l], inv2)
        x2d = _down(hid, w2[l], x1)
    return x2d.reshape(bsz, seq, d)
```

```python
import functools
import math

import jax
import jax.numpy as jnp
import numpy as np
from jax import lax
from jax.experimental import pallas as pl
from jax.experimental.pallas import tpu as pltpu

HEAD_DIM = 128
N_Q_HEADS = 16
N_KV_HEADS = 4
GQA_GROUP = N_Q_HEADS // N_KV_HEADS
ATTN_W = N_Q_HEADS * HEAD_DIM
KV_W = N_KV_HEADS * HEAD_DIM
WINDOW = 128
BLOCK = 128
N_BUCKETS = 32
MAX_DISTANCE = 128
GMLP_HEADS = 16
GMLP_W = GMLP_HEADS * HEAD_DIM
EPS = 1e-6
NEG = -1e30
LOG2E = math.log2(math.e)

F32 = jnp.float32
BF16 = jnp.bfloat16

V7X_VMEM_LIMIT_BYTES = 60 * 1024 * 1024


def _params(sem):
    return pltpu.CompilerParams(dimension_semantics=sem,
                                vmem_limit_bytes=V7X_VMEM_LIMIT_BYTES)


def _prep_kernel(x_ref, g_ref, xg_ref, inv_ref):
    x = x_ref[...]
    d = x.shape[-1]
    ss = jnp.sum(x * x, axis=-1, keepdims=True)
    inv_ref[...] = lax.rsqrt(ss * (1.0 / d) + EPS)
    xg_ref[...] = (x * g_ref[...]).astype(BF16)


def _prep(x2d, gain, n_rows, rows=256):
    m, d = n_rows, x2d.shape[1]
    return pl.pallas_call(
        _prep_kernel,
        grid=(m // rows,),
        in_specs=[pl.BlockSpec((rows, d), lambda i: (i, 0)),
                  pl.BlockSpec((1, d), lambda i: (0, 0))],
        out_specs=[pl.BlockSpec((rows, d), lambda i: (i, 0)),
                   pl.BlockSpec((rows, 1), lambda i: (i, 0))],
        out_shape=[jax.ShapeDtypeStruct((m, d), BF16),
                   jax.ShapeDtypeStruct((m, 1), F32)],
        compiler_params=_params(("arbitrary",)),
        name="prep_norm",
    )(x2d, gain.reshape(1, d))


FIRST_TN = 512


def _proj_kernel(a_ref, w_ref, *refs, body, n_in, n_prev, n_out, cast_w, scale_w, act_from):
    if scale_w:
        ws_ref, refs = refs[0], refs[1:]
    ins = refs[:n_in]
    outs = refs[n_in + n_prev:n_in + n_prev + n_out]
    rest = refs[n_in + n_prev + n_out:]
    if cast_w:
        wb_ref, rest = rest[0], rest[1:]

    def run(**body_kw):
        if cast_w:
            w = w_ref[...]
            if scale_w:
                w = w * ws_ref[...]
            w = w.astype(BF16)
            wb_ref[...] = w
        else:
            w = w_ref[...]
        acc = jnp.dot(a_ref[...], w, preferred_element_type=F32)
        body(acc, ins, outs, rest, **body_kw)

    if act_from is None:
        run()
    else:
        j = pl.program_id(1)
        pl.when(j < act_from)(functools.partial(run, act=False))
        pl.when(j >= act_from)(functools.partial(run, act=True))


def _project(body, name, a, w, ins, outs, *, tm, tn, prev=None, scratch=(), w_scale=None,
             act_col=None, first_rows=None, single_buffer_a=False):
    m, k = a.shape
    n = w.shape[1]
    first = prev is None
    if first_rows is None:
        first_rows = tm
    ioff = 0 if first else first_rows // tm
    ni = 1 if first else (m - first_rows) // tm

    def spec(kind):
        if kind == "row":
            return pl.BlockSpec((tm, 1), lambda i, j: (i + ioff, 0))
        if kind == "col":
            return pl.BlockSpec((1, tn), lambda i, j: (0, j))
        return pl.BlockSpec((tm, tn), lambda i, j: (i + ioff, j))

    a_mode = dict(pipeline_mode=pl.Buffered(1)) if single_buffer_a else {}
    in_specs = [pl.BlockSpec((tm, k), lambda i, j: (i + ioff, 0), **a_mode),
                pl.BlockSpec((k, tn), lambda i, j: (0, j))]
    args = [a, w]
    scale_w = first and w_scale is not None
    if scale_w:
        in_specs.append(pl.BlockSpec((k, 1), lambda i, j: (0, 0), pipeline_mode=pl.Buffered(1)))
        args.append(w_scale)
    in_specs += [spec(kind) for _, kind in ins]
    args += [arr for arr, _ in ins]
    out_specs = [spec(kind) for _, kind in outs]
    out_shape = [sds for sds, _ in outs]
    aliases = {}
    if first:
        out_specs.append(pl.BlockSpec((k, tn), lambda i, j: (0, j)))
        out_shape.append(jax.ShapeDtypeStruct((k, n), BF16))
    else:
        for idx, p in enumerate(prev):
            aliases[len(args)] = idx
            in_specs.append(pl.BlockSpec(memory_space=pl.ANY))
            args.append(p)
    assert act_col is None or act_col % tn == 0
    return pl.pallas_call(
        functools.partial(_proj_kernel, body=body, n_in=len(ins), n_prev=len(aliases),
                          n_out=len(outs), cast_w=first, scale_w=scale_w,
                          act_from=None if act_col is None else act_col // tn),
        grid=(ni, n // tn),
        in_specs=in_specs,
        out_specs=out_specs,
        out_shape=out_shape,
        input_output_aliases=aliases,
        scratch_shapes=list(scratch),
        compiler_params=_params(("arbitrary", "arbitrary")),
        name=name + ("_first" if first else "_rest"),
    )(*args)


def _two_pass(body, name, a, w_f32, ins, outs, *, tm, tn, scratch=(), w_scale=None, first_rows=None,
              first_tn=FIRST_TN, single_buffer_a=False):
    first_rows = first_rows or tm
    scratch_for = scratch if callable(scratch) else (lambda rows: scratch)
    *part, wb = _project(body, name, a, w_f32, ins, outs, tm=first_rows, tn=first_tn,
                         scratch=scratch_for(first_rows), w_scale=w_scale,
                         single_buffer_a=single_buffer_a)
    return _project(body, name, a, wb, ins, outs, tm=tm, tn=tn, prev=part, scratch=scratch_for(tm),
                    first_rows=first_rows)


def _gelu_tanh(x):
    c = math.sqrt(2.0 / math.pi)
    inner = x * (c + (c * 0.044715) * (x * x))
    return (0.5 * x) * (1.0 + jnp.tanh(inner))


def _scale_body(acc, ins, outs, scratch, act=False):
    (inv_ref,), (o_ref,) = ins, outs
    t = acc * inv_ref[...]
    o_ref[...] = (_gelu_tanh(t) if act else t).astype(o_ref.dtype)


def _relu2_body(acc, ins, outs, scratch):
    (inv_ref,), (o_ref,) = ins, outs
    r = jnp.maximum(acc * inv_ref[...], 0.0)
    o_ref[...] = (r * r).astype(o_ref.dtype)


def _residual_norm_body(acc, ins, outs, scratch, *, d_model):
    (x_ref, g_ref), (x1_ref, x1g_ref, inv_ref), (ss_ref,) = ins, outs, scratch
    j = pl.program_id(1)
    x1 = x_ref[...] + acc
    x1_ref[...] = x1
    x1g_ref[...] = (x1 * g_ref[...]).astype(BF16)
    part = jnp.sum(x1 * x1, axis=-1, keepdims=True)

    @pl.when(j == 0)
    def _():
        ss_ref[...] = part

    @pl.when(j > 0)
    def _():
        ss_ref[...] += part

    @pl.when(j == pl.num_programs(1) - 1)
    def _():
        inv_ref[...] = lax.rsqrt(ss_ref[...] * (1.0 / d_model) + EPS)


PREP_ROWS = 256


def _inproj_rest_kernel(x_ref, g_ref, w_ref, z_prev_ref, o_ref, xg_scr, inv_scr,
                        *, n_grid_rows, prep_steps, act_from):
    del z_prev_ref
    r = pl.program_id(0)
    j = pl.program_id(1)
    slot_in, slot_out = lax.rem(r, 2), lax.rem(r + 1, 2)

    def project(act):
        acc = jnp.dot(xg_scr[slot_in], w_ref[...], preferred_element_type=F32)
        t = acc * inv_scr[slot_in]
        o_ref[...] = (_gelu_tanh(t) if act else t).astype(o_ref.dtype)

    def prep():
        x = x_ref[...]
        rows = pl.ds(pl.multiple_of(j * PREP_ROWS, PREP_ROWS), PREP_ROWS)
        ss = jnp.sum(x * x, axis=-1, keepdims=True)
        inv_scr[slot_out, rows, :] = lax.rsqrt(ss * (1.0 / x.shape[-1]) + EPS)
        xg_scr[slot_out, rows, :] = (x * g_ref[...]).astype(BF16)

    has_next = r < n_grid_rows - 1
    prepping = has_next & (j < prep_steps)

    @pl.when((r == 0) & prepping)
    def _():
        prep()

    for act in (False, True):
        variant = (r > 0) & ((j >= act_from) == act)

        @pl.when(variant & prepping)
        def _():
            project(act)
            prep()

        @pl.when(variant & jnp.logical_not(prepping))
        def _():
            project(act)


def _inproj(x2d, gain, w_in, act_col, tm=1024, tn=1024, first_blocks=1):
    m, k = x2d.shape
    n = w_in.shape[1]
    ni, nj = m // tm, n // tn
    prep_steps = tm // PREP_ROWS
    assert prep_steps <= nj and act_col % tn == 0 and first_blocks < ni
    first_rows = first_blocks * tm
    xg0, inv0 = _prep(x2d, gain, first_rows)
    z_part, wb = _project(_scale_body, "in_proj", xg0, w_in, [(inv0, "row")],
                          [(jax.ShapeDtypeStruct((m, n), BF16), "tile")], tm=first_rows,
                          tn=FIRST_TN, act_col=act_col)

    n_grid_rows = ni - first_blocks + 1

    def x_map(r, j):
        return (jnp.minimum(first_blocks + r, ni - 1) * prep_steps
                + jnp.minimum(j, prep_steps - 1), 0)

    def col(r, j):
        return jnp.where(r == 0, 0, j)

    return pl.pallas_call(
        functools.partial(_inproj_rest_kernel, n_grid_rows=n_grid_rows, prep_steps=prep_steps,
                          act_from=act_col // tn),
        grid=(n_grid_rows, nj),
        in_specs=[pl.BlockSpec((PREP_ROWS, k), x_map),
                  pl.BlockSpec((1, k), lambda r, j: (0, 0)),
                  pl.BlockSpec((k, tn), lambda r, j: (0, col(r, j))),
                  pl.BlockSpec(memory_space=pl.ANY)],
        out_specs=pl.BlockSpec((tm, tn),
                               lambda r, j: (first_blocks + jnp.maximum(r, 1) - 1, col(r, j))),
        out_shape=jax.ShapeDtypeStruct((m, n), BF16),
        input_output_aliases={3: 0},
        scratch_shapes=[pltpu.VMEM((2, tm, k), BF16), pltpu.VMEM((2, tm, 1), F32)],
        compiler_params=_params(("arbitrary", "arbitrary")),
        name="in_proj_rest",
    )(x2d, gain.reshape(1, k), wb, z_part)


def _up(x1g, w1, inv, tm=1024, tn=1024):
    m, n = x1g.shape[0], w1.shape[1]
    (hid,) = _two_pass(_relu2_body, "mlp_up", x1g, w1, [(inv, "row")],
                       [(jax.ShapeDtypeStruct((m, n), BF16), "tile")], tm=tm, tn=tn)
    return hid


def _outproj(mix, w_out, mix_gain, x2d, gain, tm=1024, tn=512):
    m, n = x2d.shape
    outs = [(jax.ShapeDtypeStruct((m, n), F32), "tile"),
            (jax.ShapeDtypeStruct((m, n), BF16), "tile"),
            (jax.ShapeDtypeStruct((m, 1), F32), "row")]
    return _two_pass(functools.partial(_residual_norm_body, d_model=n), "out_proj", mix, w_out,
                     [(x2d, "tile"), (gain.reshape(1, n), "col")], outs, tm=tm, tn=tn,
                     scratch=[pltpu.VMEM((tm, 1), F32)], w_scale=mix_gain.reshape(-1, 1),
                     single_buffer_a=True)


def _down_kernel(a_ref, w_ref, x1_ref, *refs, cast_w):
    o_ref = refs[0] if cast_w else refs[1]

    @pl.when(pl.program_id(2) == 0)
    def _():
        o_ref[...] = x1_ref[...]

    if cast_w:
        wb_ref = refs[1]
        w = w_ref[...].astype(BF16)
        wb_ref[...] = w
    else:
        w = w_ref[...]
    o_ref[...] += jnp.dot(a_ref[...], w, preferred_element_type=F32)


def _down_call(a, w, x1, *, tm, tn, tk, prev=None):
    m, k = a.shape
    n = w.shape[1]
    first = prev is None
    ioff = 0 if first else 1
    ni = 1 if first else m // tm - 1
    in_specs = [pl.BlockSpec((tm, tk), lambda i, j, kk: (i + ioff, kk)),
                pl.BlockSpec((tk, tn), lambda i, j, kk: (kk, j)),
                pl.BlockSpec((tm, tn), lambda i, j, kk: (i + ioff, j))]
    args = [a, w, x1]
    out_specs = [pl.BlockSpec((tm, tn), lambda i, j, kk: (i + ioff, j))]
    out_shape = [jax.ShapeDtypeStruct((m, n), F32)]
    aliases = {}
    if first:
        out_specs.append(pl.BlockSpec((tk, tn), lambda i, j, kk: (kk, j)))
        out_shape.append(jax.ShapeDtypeStruct((k, n), BF16))
    else:
        aliases[len(args)] = 0
        in_specs.append(pl.BlockSpec(memory_space=pl.ANY))
        args.append(prev)
    return pl.pallas_call(
        functools.partial(_down_kernel, cast_w=first),
        grid=(ni, n // tn, k // tk),
        in_specs=in_specs,
        out_specs=out_specs,
        out_shape=out_shape,
        input_output_aliases=aliases,
        compiler_params=_params(("arbitrary", "arbitrary", "arbitrary")),
        name="mlp_down" + ("_first" if first else "_rest"),
    )(*args)


def _down(hid, w2, x1, tm=1024, tn=1024, tk=4096):
    part, wb = _down_call(hid, w2, x1, tm=tm, tn=tn, tk=tk // 2)
    (y,) = _down_call(hid, wb, x1, tm=tm, tn=tn, tk=tk, prev=part)
    return y


def _t5_bucket(rel):
    nb = N_BUCKETS // 2
    max_exact = nb // 2
    ret = (rel > 0).astype(np.int32) * nb
    n = np.abs(rel)
    large = max_exact + (np.log(np.maximum(n, 1).astype(np.float32) / max_exact)
                         / math.log(MAX_DISTANCE / max_exact) * (nb - max_exact)).astype(np.int32)
    large = np.minimum(large, nb - 1)
    return ret + np.where(n < max_exact, n, large)


def _bucket_table():
    a = np.arange(BLOCK)[:, None]
    s = np.arange(3 * BLOCK)[None, :]
    rel = s - BLOCK - a
    return np.where(np.abs(rel) <= WINDOW, _t5_bucket(rel), -1).astype(np.int32)


def _bias_kernel(rb_ref, bk_ref, o_ref):
    h = pl.program_id(0)
    bk = bk_ref[...]
    acc = jnp.full(bk.shape, NEG, F32)
    for b in range(N_BUCKETS):
        acc = jnp.where(bk == b, rb_ref[b, h] * LOG2E, acc)
    col = lax.broadcasted_iota(jnp.int32, bk.shape, 1)
    o_ref[0] = jnp.where(col >= BLOCK, acc, NEG)
    o_ref[1] = acc
    o_ref[2] = jnp.where(col < 2 * BLOCK, acc, NEG)


def _bias_tables(rel_bias):
    return pl.pallas_call(
        _bias_kernel,
        grid=(N_Q_HEADS,),
        in_specs=[pl.BlockSpec(memory_space=pltpu.SMEM),
                  pl.BlockSpec((BLOCK, 3 * BLOCK), lambda h: (0, 0))],
        out_specs=pl.BlockSpec((3, None, BLOCK, 3 * BLOCK), lambda h: (0, h, 0, 0)),
        out_shape=jax.ShapeDtypeStruct((3, N_Q_HEADS, BLOCK, 3 * BLOCK), F32),
        compiler_params=_params(("arbitrary",)),
        name="bias_tables",
    )(rel_bias, jnp.asarray(_bucket_table()))


def _rms_scale(t):
    return lax.rsqrt(jnp.sum(t * t, axis=-1, keepdims=True) * (1.0 / t.shape[-1]) + EPS)


MIX_BLOCKS = 2


def _mix_kernel(sink_ref, q_ref, kp_ref, ko_ref, kn_ref, vp_ref, vo_ref, vn_ref,
                u0_ref, u1_ref, g0_ref, g1_ref, bias_ref, qg_ref, kg_ref,
                vg_ref, ws_ref, bt_ref, mix_ref, a_scr, g_scr):
    n = pl.program_id(1)
    nlast = pl.num_programs(1) - 1
    kfold = kg_ref[...] * qg_ref[...] * (HEAD_DIM ** -0.5 * LOG2E)

    kband = jnp.concatenate([kp_ref[...], ko_ref[...], kn_ref[...]], axis=0).astype(F32)
    vband = jnp.concatenate([vp_ref[...], vo_ref[...], vn_ref[...]], axis=0)
    ones = jnp.ones((3 * BLOCK, HEAD_DIM), BF16)
    for kh in range(N_KV_HEADS):
        k = kband[:, kh * HEAD_DIM:(kh + 1) * HEAD_DIM]
        kn_all = (k * _rms_scale(k) * kfold).astype(BF16)
        for sb in range(MIX_BLOCKS):
            tok = slice(sb * BLOCK, (sb + 1) * BLOCK)
            band = slice(sb * BLOCK, (sb + 3) * BLOCK)
            if sb == 0:
                variant = jnp.where(n == 0, 0, 1)
            elif sb == MIX_BLOCKS - 1:
                variant = jnp.where(n == nlast, 2, 1)
            else:
                variant = 1
            kn = kn_all[band]
            v1 = jnp.concatenate([vband[band, kh * HEAD_DIM:(kh + 1) * HEAD_DIM], ones], axis=1)
            qs = []
            for g in range(GQA_GROUP):
                h = kh * GQA_GROUP + g
                qh = q_ref[tok, h * HEAD_DIM:(h + 1) * HEAD_DIM].astype(F32)
                qs.append((qh * _rms_scale(qh)).astype(BF16))
            qstack = jnp.concatenate(qs, axis=0)
            s = lax.dot_general(qstack, kn, (((1,), (1,)), ((), ())),
                                preferred_element_type=F32)
            ps, sink_terms = [], []
            for g in range(GQA_GROUP):
                h = kh * GQA_GROUP + g
                sink = sink_ref[h] * LOG2E
                sg = s[g * BLOCK:(g + 1) * BLOCK] + bias_ref[variant, h]
                m = jnp.maximum(jnp.max(sg, axis=-1, keepdims=True), sink)
                ps.append(jnp.exp2(sg - m).astype(BF16))
                sink_terms.append(jnp.exp2(sink - m))
            pstack = jnp.concatenate(ps, axis=0)
            ol = jnp.dot(pstack, v1, preferred_element_type=F32)
            for g in range(GQA_GROUP):
                h = kh * GQA_GROUP + g
                rows = slice(g * BLOCK, (g + 1) * BLOCK)
                denom = ol[rows, HEAD_DIM:] + sink_terms[g]
                a_scr[tok, h * HEAD_DIM:(h + 1) * HEAD_DIM] = ol[rows, :HEAD_DIM] * (1.0 / denom)

    a = a_scr[...]
    mix_ref[:, :ATTN_W] = (a * _rms_scale(a)).astype(mix_ref.dtype)

    gu = jnp.concatenate([u0_ref[...], u1_ref[...]], axis=1).astype(F32)
    gv = jnp.concatenate([g0_ref[...], g1_ref[...]], axis=1).astype(F32)
    vn_ = (gv * _rms_scale(gv) * vg_ref[...]).astype(BF16)
    for sb in range(MIX_BLOCKS):
        tok = slice(sb * BLOCK, (sb + 1) * BLOCK)
        for h in range(GMLP_HEADS):
            sl = slice(h * HEAD_DIM, (h + 1) * HEAD_DIM)
            sv = jnp.dot(ws_ref[h], vn_[tok, sl], preferred_element_type=F32) + bt_ref[:, h:h + 1]
            g_scr[tok, sl] = gu[tok, sl] * sv
    gg = g_scr[...]
    mix_ref[:, ATTN_W:] = (gg * _rms_scale(gg)).astype(mix_ref.dtype)


def _mix(z3, bias, sink, q_gain, k_gain, v_gain, w_s, b_t):
    bsz, seq, _ = z3.shape
    nblk = seq // BLOCK
    rows = MIX_BLOCKS * BLOCK
    assert nblk >= 2 and nblk % MIX_BLOCKS == 0
    kcol = ATTN_W // KV_W
    vcol = kcol + 1
    ucol = (ATTN_W + 2 * KV_W) // 1024
    gcol = ucol + 2

    def own(width, col):
        return pl.BlockSpec((None, rows, width), lambda b, n: (b, n, col))

    def edge(width, col, after):
        def imap(b, n):
            blk = n * MIX_BLOCKS + (MIX_BLOCKS if after else -1)
            return (b, jnp.clip(blk, 0, nblk - 1), col)
        return pl.BlockSpec((None, BLOCK, width), imap)

    def full(shape):
        return pl.BlockSpec(shape, lambda b, n: (0,) * len(shape))

    in_specs = [
        pl.BlockSpec(memory_space=pltpu.SMEM),
        own(ATTN_W, 0),
        edge(KV_W, kcol, False), own(KV_W, kcol), edge(KV_W, kcol, True),
        edge(KV_W, vcol, False), own(KV_W, vcol), edge(KV_W, vcol, True),
        own(1024, ucol), own(1024, ucol + 1),
        own(1024, gcol), own(1024, gcol + 1),
        pl.BlockSpec(bias.shape, lambda b, n: (0, 0, 0, 0)),
        full((1, HEAD_DIM)), full((1, HEAD_DIM)),
        full((1, GMLP_W)), full(w_s.shape), full(b_t.shape),
    ]
    return pl.pallas_call(
        _mix_kernel,
        grid=(bsz, nblk // MIX_BLOCKS),
        in_specs=in_specs,
        out_specs=pl.BlockSpec((None, rows, ATTN_W + GMLP_W), lambda b, n: (b, n, 0)),
        out_shape=jax.ShapeDtypeStruct((bsz, seq, ATTN_W + GMLP_W), BF16),
        scratch_shapes=[pltpu.VMEM((rows, ATTN_W), F32), pltpu.VMEM((rows, GMLP_W), F32)],
        compiler_params=_params(("arbitrary", "arbitrary")),
        name="mix",
    )(sink, z3, z3, z3, z3, z3, z3, z3, z3, z3, z3, z3, bias,
      q_gain.reshape(1, HEAD_DIM), k_gain.reshape(1, HEAD_DIM),
      v_gain.reshape(1, GMLP_W), w_s, b_t)


def kernel(x, norm1, w_in, q_gain, k_gain, rel_bias, attn_sink, attn_out_gain,
           gmlp_v_gain, gmlp_w_s, gmlp_b_s, gmlp_out_gain, w_out, norm2, w1, w2):
    bsz, seq, d = x.shape
    depth = norm1.shape[0]
    bias = _bias_tables(rel_bias)
    x2d = x.reshape(bsz * seq, d)
    for l in range(depth):
        z = _inproj(x2d, norm1[l], w_in[l], act_col=ATTN_W + 2 * KV_W)
        mix = _mix(z.reshape(bsz, seq, -1), bias, attn_sink[l], q_gain[l], k_gain[l],
                   gmlp_v_gain[l], gmlp_w_s[l].astype(BF16), gmlp_b_s[l].T)
        mix_gain = jnp.concatenate([attn_out_gain[l], gmlp_out_gain[l]])
        x1, x1g, inv2 = _outproj(mix.reshape(bsz * seq, -1), w_out[l], mix_gain, x2d, norm2[l])
        hid = _up(x1g, w1[l], inv2)
        x2d = _down(hid, w2[l], x1)
    return x2d.reshape(bsz, seq, d)
```

```python
import functools
import math

import jax
import jax.numpy as jnp
import numpy as np
from jax import lax
from jax.experimental import pallas as pl
from jax.experimental.pallas import tpu as pltpu

HEAD_DIM = 128
N_Q_HEADS = 16
N_KV_HEADS = 4
GQA_GROUP = N_Q_HEADS // N_KV_HEADS
ATTN_W = N_Q_HEADS * HEAD_DIM
KV_W = N_KV_HEADS * HEAD_DIM
WINDOW = 128
BLOCK = 128
N_BUCKETS = 32
MAX_DISTANCE = 128
GMLP_HEADS = 16
GMLP_W = GMLP_HEADS * HEAD_DIM
EPS = 1e-6
NEG = -1e30
LOG2E = math.log2(math.e)

F32 = jnp.float32
BF16 = jnp.bfloat16

V7X_VMEM_LIMIT_BYTES = 60 * 1024 * 1024


def _params(sem):
    return pltpu.CompilerParams(dimension_semantics=sem,
                                vmem_limit_bytes=V7X_VMEM_LIMIT_BYTES)


def _prep_kernel(x_ref, g_ref, xg_ref, inv_ref):
    x = x_ref[...]
    d = x.shape[-1]
    ss = jnp.sum(x * x, axis=-1, keepdims=True)
    inv_ref[...] = lax.rsqrt(ss * (1.0 / d) + EPS)
    xg_ref[...] = (x * g_ref[...]).astype(BF16)


def _prep(x2d, gain, n_rows, rows=256):
    m, d = n_rows, x2d.shape[1]
    return pl.pallas_call(
        _prep_kernel,
        grid=(m // rows,),
        in_specs=[pl.BlockSpec((rows, d), lambda i: (i, 0)),
                  pl.BlockSpec((1, d), lambda i: (0, 0))],
        out_specs=[pl.BlockSpec((rows, d), lambda i: (i, 0)),
                   pl.BlockSpec((rows, 1), lambda i: (i, 0))],
        out_shape=[jax.ShapeDtypeStruct((m, d), BF16),
                   jax.ShapeDtypeStruct((m, 1), F32)],
        compiler_params=_params(("arbitrary",)),
        name="prep_norm",
    )(x2d, gain.reshape(1, d))


FIRST_TN = 512


def _proj_kernel(a_ref, w_ref, *refs, body, n_in, n_prev, n_out, cast_w, scale_w, act_from):
    if scale_w:
        ws_ref, refs = refs[0], refs[1:]
    ins = refs[:n_in]
    outs = refs[n_in + n_prev:n_in + n_prev + n_out]
    rest = refs[n_in + n_prev + n_out:]
    if cast_w:
        wb_ref, rest = rest[0], rest[1:]

    def run(**body_kw):
        if cast_w:
            w = w_ref[...]
            if scale_w:
                w = w * ws_ref[...]
            w = w.astype(BF16)
            wb_ref[...] = w
        else:
            w = w_ref[...]
        acc = jnp.dot(a_ref[...], w, preferred_element_type=F32)
        body(acc, ins, outs, rest, **body_kw)

    if act_from is None:
        run()
    else:
        j = pl.program_id(1)
        pl.when(j < act_from)(functools.partial(run, act=False))
        pl.when(j >= act_from)(functools.partial(run, act=True))


def _project(body, name, a, w, ins, outs, *, tm, tn, prev=None, scratch=(), w_scale=None,
             act_col=None):
    m, k = a.shape
    n = w.shape[1]
    first = prev is None
    ioff = 0 if first else 1
    ni = 1 if first else m // tm - 1

    def spec(kind):
        if kind == "row":
            return pl.BlockSpec((tm, 1), lambda i, j: (i + ioff, 0))
        if kind == "col":
            return pl.BlockSpec((1, tn), lambda i, j: (0, j))
        return pl.BlockSpec((tm, tn), lambda i, j: (i + ioff, j))

    a_mode = dict(pipeline_mode=pl.Buffered(1)) if first else {}
    in_specs = [pl.BlockSpec((tm, k), lambda i, j: (i + ioff, 0), **a_mode),
                pl.BlockSpec((k, tn), lambda i, j: (0, j))]
    args = [a, w]
    scale_w = first and w_scale is not None
    if scale_w:
        in_specs.append(pl.BlockSpec((k, 1), lambda i, j: (0, 0), pipeline_mode=pl.Buffered(1)))
        args.append(w_scale)
    in_specs += [spec(kind) for _, kind in ins]
    args += [arr for arr, _ in ins]
    out_specs = [spec(kind) for _, kind in outs]
    out_shape = [sds for sds, _ in outs]
    aliases = {}
    if first:
        out_specs.append(pl.BlockSpec((k, tn), lambda i, j: (0, j)))
        out_shape.append(jax.ShapeDtypeStruct((k, n), BF16))
    else:
        for idx, p in enumerate(prev):
            aliases[len(args)] = idx
            in_specs.append(pl.BlockSpec(memory_space=pl.ANY))
            args.append(p)
    assert act_col is None or act_col % tn == 0
    return pl.pallas_call(
        functools.partial(_proj_kernel, body=body, n_in=len(ins), n_prev=len(aliases),
                          n_out=len(outs), cast_w=first, scale_w=scale_w,
                          act_from=None if act_col is None else act_col // tn),
        grid=(ni, n // tn),
        in_specs=in_specs,
        out_specs=out_specs,
        out_shape=out_shape,
        input_output_aliases=aliases,
        scratch_shapes=list(scratch),
        compiler_params=_params(("arbitrary", "arbitrary")),
        name=name + ("_first" if first else "_rest"),
    )(*args)


def _two_pass(body, name, a, w_f32, ins, outs, *, tm, tn, scratch=(), w_scale=None):
    *part, wb = _project(body, name, a, w_f32, ins, outs, tm=tm, tn=FIRST_TN, scratch=scratch,
                         w_scale=w_scale)
    return _project(body, name, a, wb, ins, outs, tm=tm, tn=tn, prev=part, scratch=scratch)


def _gelu_tanh(x):
    c = math.sqrt(2.0 / math.pi)
    inner = x * (c + (c * 0.044715) * (x * x))
    return (0.5 * x) * (1.0 + jnp.tanh(inner))


def _scale_body(acc, ins, outs, scratch, act=False):
    (inv_ref,), (o_ref,) = ins, outs
    t = acc * inv_ref[...]
    o_ref[...] = (_gelu_tanh(t) if act else t).astype(o_ref.dtype)


def _relu2_body(acc, ins, outs, scratch):
    (inv_ref,), (o_ref,) = ins, outs
    r = jnp.maximum(acc * inv_ref[...], 0.0)
    o_ref[...] = (r * r).astype(o_ref.dtype)


def _residual_norm_body(acc, ins, outs, scratch, *, d_model):
    (x_ref, g_ref), (x1_ref, x1g_ref, inv_ref), (ss_ref,) = ins, outs, scratch
    j = pl.program_id(1)
    x1 = x_ref[...] + acc
    x1_ref[...] = x1
    x1g_ref[...] = (x1 * g_ref[...]).astype(BF16)
    part = jnp.sum(x1 * x1, axis=-1, keepdims=True)

    @pl.when(j == 0)
    def _():
        ss_ref[...] = part

    @pl.when(j > 0)
    def _():
        ss_ref[...] += part

    @pl.when(j == pl.num_programs(1) - 1)
    def _():
        inv_ref[...] = lax.rsqrt(ss_ref[...] * (1.0 / d_model) + EPS)


PREP_ROWS = 256


def _inproj_rest_kernel(x_ref, g_ref, w_ref, z_prev_ref, o_ref, xg_scr, inv_scr,
                        *, n_row_blocks, prep_steps, act_from):
    del z_prev_ref
    r = pl.program_id(0)
    j = pl.program_id(1)
    slot_in, slot_out = lax.rem(r, 2), lax.rem(r + 1, 2)

    def project(act):
        acc = jnp.dot(xg_scr[slot_in], w_ref[...], preferred_element_type=F32)
        t = acc * inv_scr[slot_in]
        o_ref[...] = (_gelu_tanh(t) if act else t).astype(o_ref.dtype)

    def prep():
        x = x_ref[...]
        rows = pl.ds(pl.multiple_of(j * PREP_ROWS, PREP_ROWS), PREP_ROWS)
        ss = jnp.sum(x * x, axis=-1, keepdims=True)
        inv_scr[slot_out, rows, :] = lax.rsqrt(ss * (1.0 / x.shape[-1]) + EPS)
        xg_scr[slot_out, rows, :] = (x * g_ref[...]).astype(BF16)

    has_next = r < n_row_blocks - 1
    prepping = has_next & (j < prep_steps)

    @pl.when((r == 0) & prepping)
    def _():
        prep()

    for act in (False, True):
        variant = (r > 0) & ((j >= act_from) == act)

        @pl.when(variant & prepping)
        def _():
            project(act)
            prep()

        @pl.when(variant & jnp.logical_not(prepping))
        def _():
            project(act)


def _inproj(x2d, gain, w_in, act_col, tm=1024, tn=1024):
    m, k = x2d.shape
    n = w_in.shape[1]
    ni, nj = m // tm, n // tn
    prep_steps = tm // PREP_ROWS
    assert prep_steps <= nj and act_col % tn == 0
    xg0, inv0 = _prep(x2d, gain, tm)
    z_part, wb = _project(_scale_body, "in_proj", xg0, w_in, [(inv0, "row")],
                          [(jax.ShapeDtypeStruct((m, n), BF16), "tile")], tm=tm, tn=FIRST_TN,
                          act_col=act_col)

    def x_map(r, j):
        return (jnp.minimum(r + 1, ni - 1) * prep_steps + jnp.minimum(j, prep_steps - 1), 0)

    def col(r, j):
        return jnp.where(r == 0, 0, j)

    return pl.pallas_call(
        functools.partial(_inproj_rest_kernel, n_row_blocks=ni, prep_steps=prep_steps,
                          act_from=act_col // tn),
        grid=(ni, nj),
        in_specs=[pl.BlockSpec((PREP_ROWS, k), x_map),
                  pl.BlockSpec((1, k), lambda r, j: (0, 0)),
                  pl.BlockSpec((k, tn), lambda r, j: (0, col(r, j))),
                  pl.BlockSpec(memory_space=pl.ANY)],
        out_specs=pl.BlockSpec((tm, tn), lambda r, j: (jnp.maximum(r, 1), col(r, j))),
        out_shape=jax.ShapeDtypeStruct((m, n), BF16),
        input_output_aliases={3: 0},
        scratch_shapes=[pltpu.VMEM((2, tm, k), BF16), pltpu.VMEM((2, tm, 1), F32)],
        compiler_params=_params(("arbitrary", "arbitrary")),
        name="in_proj_rest",
    )(x2d, gain.reshape(1, k), wb, z_part)


def _up(x1g, w1, inv, tm=1024, tn=1024):
    m, n = x1g.shape[0], w1.shape[1]
    (hid,) = _two_pass(_relu2_body, "mlp_up", x1g, w1, [(inv, "row")],
                       [(jax.ShapeDtypeStruct((m, n), BF16), "tile")], tm=tm, tn=tn)
    return hid


def _outproj(mix, w_out, mix_gain, x2d, gain, tm=1024, tn=512):
    m, n = x2d.shape
    outs = [(jax.ShapeDtypeStruct((m, n), F32), "tile"),
            (jax.ShapeDtypeStruct((m, n), BF16), "tile"),
            (jax.ShapeDtypeStruct((m, 1), F32), "row")]
    return _two_pass(functools.partial(_residual_norm_body, d_model=n), "out_proj", mix, w_out,
                     [(x2d, "tile"), (gain.reshape(1, n), "col")], outs, tm=tm, tn=tn,
                     scratch=[pltpu.VMEM((tm, 1), F32)], w_scale=mix_gain.reshape(-1, 1))


def _down_kernel(a_ref, w_ref, x1_ref, *refs, cast_w):
    o_ref = refs[0] if cast_w else refs[1]

    @pl.when(pl.program_id(2) == 0)
    def _():
        o_ref[...] = x1_ref[...]

    if cast_w:
        wb_ref = refs[1]
        w = w_ref[...].astype(BF16)
        wb_ref[...] = w
    else:
        w = w_ref[...]
    o_ref[...] += jnp.dot(a_ref[...], w, preferred_element_type=F32)


def _down_call(a, w, x1, *, tm, tn, tk, prev=None):
    m, k = a.shape
    n = w.shape[1]
    first = prev is None
    ioff = 0 if first else 1
    ni = 1 if first else m // tm - 1
    in_specs = [pl.BlockSpec((tm, tk), lambda i, j, kk: (i + ioff, kk)),
                pl.BlockSpec((tk, tn), lambda i, j, kk: (kk, j)),
                pl.BlockSpec((tm, tn), lambda i, j, kk: (i + ioff, j))]
    args = [a, w, x1]
    out_specs = [pl.BlockSpec((tm, tn), lambda i, j, kk: (i + ioff, j))]
    out_shape = [jax.ShapeDtypeStruct((m, n), F32)]
    aliases = {}
    if first:
        out_specs.append(pl.BlockSpec((tk, tn), lambda i, j, kk: (kk, j)))
        out_shape.append(jax.ShapeDtypeStruct((k, n), BF16))
    else:
        aliases[len(args)] = 0
        in_specs.append(pl.BlockSpec(memory_space=pl.ANY))
        args.append(prev)
    return pl.pallas_call(
        functools.partial(_down_kernel, cast_w=first),
        grid=(ni, n // tn, k // tk),
        in_specs=in_specs,
        out_specs=out_specs,
        out_shape=out_shape,
        input_output_aliases=aliases,
        compiler_params=_params(("arbitrary", "arbitrary", "arbitrary")),
        name="mlp_down" + ("_first" if first else "_rest"),
    )(*args)


def _down(hid, w2, x1, tm=1024, tn=1024, tk=4096):
    part, wb = _down_call(hid, w2, x1, tm=tm, tn=tn, tk=tk // 2)
    (y,) = _down_call(hid, wb, x1, tm=tm, tn=tn, tk=tk, prev=part)
    return y


def _t5_bucket(rel):
    nb = N_BUCKETS // 2
    max_exact = nb // 2
    ret = (rel > 0).astype(np.int32) * nb
    n = np.abs(rel)
    large = max_exact + (np.log(np.maximum(n, 1).astype(np.float32) / max_exact)
                         / math.log(MAX_DISTANCE / max_exact) * (nb - max_exact)).astype(np.int32)
    large = np.minimum(large, nb - 1)
    return ret + np.where(n < max_exact, n, large)


def _bucket_table():
    a = np.arange(BLOCK)[:, None]
    s = np.arange(3 * BLOCK)[None, :]
    rel = s - BLOCK - a
    return np.where(np.abs(rel) <= WINDOW, _t5_bucket(rel), -1).astype(np.int32)


def _bias_kernel(rb_ref, bk_ref, o_ref):
    h = pl.program_id(0)
    bk = bk_ref[...]
    acc = jnp.full(bk.shape, NEG, F32)
    for b in range(N_BUCKETS):
        acc = jnp.where(bk == b, rb_ref[b, h] * LOG2E, acc)
    col = lax.broadcasted_iota(jnp.int32, bk.shape, 1)
    o_ref[0] = jnp.where(col >= BLOCK, acc, NEG)
    o_ref[1] = acc
    o_ref[2] = jnp.where(col < 2 * BLOCK, acc, NEG)


def _bias_tables(rel_bias):
    return pl.pallas_call(
        _bias_kernel,
        grid=(N_Q_HEADS,),
        in_specs=[pl.BlockSpec(memory_space=pltpu.SMEM),
                  pl.BlockSpec((BLOCK, 3 * BLOCK), lambda h: (0, 0))],
        out_specs=pl.BlockSpec((3, None, BLOCK, 3 * BLOCK), lambda h: (0, h, 0, 0)),
        out_shape=jax.ShapeDtypeStruct((3, N_Q_HEADS, BLOCK, 3 * BLOCK), F32),
        compiler_params=_params(("arbitrary",)),
        name="bias_tables",
    )(rel_bias, jnp.asarray(_bucket_table()))


def _rms_scale(t):
    return lax.rsqrt(jnp.sum(t * t, axis=-1, keepdims=True) * (1.0 / t.shape[-1]) + EPS)


MIX_BLOCKS = 2
Z_COL_BLOCK = 1024


def _mix_kernel(sink_ref, q_ref, kp_ref, ko_ref, kn_ref, vp_ref, vo_ref, vn_ref,
                u0_ref, u1_ref, g0_ref, g1_ref, bias_ref, qg_ref, kg_ref,
                vg_ref, ws_ref, bt_ref, mix_ref, a_scr, g_scr):
    n = pl.program_id(1)
    nlast = pl.num_programs(1) - 1
    kfold = kg_ref[...] * qg_ref[...] * (HEAD_DIM ** -0.5 * LOG2E)

    kband = jnp.concatenate([kp_ref[...], ko_ref[...], kn_ref[...]], axis=0).astype(F32)
    vband = jnp.concatenate([vp_ref[...], vo_ref[...], vn_ref[...]], axis=0)
    ones = jnp.ones((3 * BLOCK, HEAD_DIM), BF16)
    for kh in range(N_KV_HEADS):
        k = kband[:, kh * HEAD_DIM:(kh + 1) * HEAD_DIM]
        kn_all = (k * _rms_scale(k) * kfold).astype(BF16)
        for sb in range(MIX_BLOCKS):
            tok = slice(sb * BLOCK, (sb + 1) * BLOCK)
            band = slice(sb * BLOCK, (sb + 3) * BLOCK)
            if sb == 0:
                variant = jnp.where(n == 0, 0, 1)
            elif sb == MIX_BLOCKS - 1:
                variant = jnp.where(n == nlast, 2, 1)
            else:
                variant = 1
            kn = kn_all[band]
            v1 = jnp.concatenate([vband[band, kh * HEAD_DIM:(kh + 1) * HEAD_DIM], ones], axis=1)
            qs = []
            for g in range(GQA_GROUP):
                h = kh * GQA_GROUP + g
                qh = q_ref[tok, h * HEAD_DIM:(h + 1) * HEAD_DIM].astype(F32)
                qs.append((qh * _rms_scale(qh)).astype(BF16))
            qstack = jnp.concatenate(qs, axis=0)
            s = lax.dot_general(qstack, kn, (((1,), (1,)), ((), ())),
                                preferred_element_type=F32)
            ps, sink_terms = [], []
            for g in range(GQA_GROUP):
                h = kh * GQA_GROUP + g
                sink = sink_ref[h] * LOG2E
                sg = s[g * BLOCK:(g + 1) * BLOCK] + bias_ref[variant, h]
                m = jnp.maximum(jnp.max(sg, axis=-1, keepdims=True), sink)
                ps.append(jnp.exp2(sg - m).astype(BF16))
                sink_terms.append(jnp.exp2(sink - m))
            pstack = jnp.concatenate(ps, axis=0)
            ol = jnp.dot(pstack, v1, preferred_element_type=F32)
            for g in range(GQA_GROUP):
                h = kh * GQA_GROUP + g
                rows = slice(g * BLOCK, (g + 1) * BLOCK)
                denom = ol[rows, HEAD_DIM:] + sink_terms[g]
                a_scr[tok, h * HEAD_DIM:(h + 1) * HEAD_DIM] = ol[rows, :HEAD_DIM] * (1.0 / denom)

    a = a_scr[...]
    mix_ref[:, :ATTN_W] = (a * _rms_scale(a)).astype(mix_ref.dtype)

    gu = jnp.concatenate([u0_ref[...], u1_ref[...]], axis=1).astype(F32)
    gv = jnp.concatenate([g0_ref[...], g1_ref[...]], axis=1).astype(F32)
    vn_ = (gv * _rms_scale(gv) * vg_ref[...]).astype(BF16)
    for sb in range(MIX_BLOCKS):
        tok = slice(sb * BLOCK, (sb + 1) * BLOCK)
        for h in range(GMLP_HEADS):
            sl = slice(h * HEAD_DIM, (h + 1) * HEAD_DIM)
            sv = jnp.dot(ws_ref[h], vn_[tok, sl], preferred_element_type=F32) + bt_ref[:, h:h + 1]
            g_scr[tok, sl] = gu[tok, sl] * sv
    gg = g_scr[...]
    mix_ref[:, ATTN_W:] = (gg * _rms_scale(gg)).astype(mix_ref.dtype)


def _mix(z3, bias, sink, q_gain, k_gain, v_gain, w_s, b_t):
    bsz, seq, _ = z3.shape
    nblk = seq // BLOCK
    rows = MIX_BLOCKS * BLOCK
    assert nblk >= 2 and nblk % MIX_BLOCKS == 0
    kcol = ATTN_W // KV_W
    vcol = kcol + 1
    ucol = (ATTN_W + 2 * KV_W) // Z_COL_BLOCK
    gcol = ucol + GMLP_W // Z_COL_BLOCK
    assert (ATTN_W + 2 * KV_W) % Z_COL_BLOCK == 0 and GMLP_W == 2 * Z_COL_BLOCK

    def own(width, col):
        return pl.BlockSpec((None, rows, width), lambda b, n: (b, n, col))

    def edge(width, col, after):
        def imap(b, n):
            blk = n * MIX_BLOCKS + (MIX_BLOCKS if after else -1)
            return (b, jnp.clip(blk, 0, nblk - 1), col)
        return pl.BlockSpec((None, BLOCK, width), imap)

    def full(shape):
        return pl.BlockSpec(shape, lambda b, n: (0,) * len(shape))

    in_specs = [
        pl.BlockSpec(memory_space=pltpu.SMEM),
        own(ATTN_W, 0),
        edge(KV_W, kcol, False), own(KV_W, kcol), edge(KV_W, kcol, True),
        edge(KV_W, vcol, False), own(KV_W, vcol), edge(KV_W, vcol, True),
        own(Z_COL_BLOCK, ucol), own(Z_COL_BLOCK, ucol + 1),
        own(Z_COL_BLOCK, gcol), own(Z_COL_BLOCK, gcol + 1),
        pl.BlockSpec(bias.shape, lambda b, n: (0, 0, 0, 0), pipeline_mode=pl.Buffered(1)),
        full((1, HEAD_DIM)), full((1, HEAD_DIM)),
        full((1, GMLP_W)), full(w_s.shape), full(b_t.shape),
    ]
    return pl.pallas_call(
        _mix_kernel,
        grid=(bsz, nblk // MIX_BLOCKS),
        in_specs=in_specs,
        out_specs=pl.BlockSpec((None, rows, ATTN_W + GMLP_W), lambda b, n: (b, n, 0)),
        out_shape=jax.ShapeDtypeStruct((bsz, seq, ATTN_W + GMLP_W), BF16),
        scratch_shapes=[pltpu.VMEM((rows, ATTN_W), F32), pltpu.VMEM((rows, GMLP_W), F32)],
        compiler_params=_params(("arbitrary", "arbitrary")),
        name="mix",
    )(sink, z3, z3, z3, z3, z3, z3, z3, z3, z3, z3, z3, bias,
      q_gain.reshape(1, HEAD_DIM), k_gain.reshape(1, HEAD_DIM),
      v_gain.reshape(1, GMLP_W), w_s, b_t)


def kernel(x, norm1, w_in, q_gain, k_gain, rel_bias, attn_sink, attn_out_gain,
           gmlp_v_gain, gmlp_w_s, gmlp_b_s, gmlp_out_gain, w_out, norm2, w1, w2):
    bsz, seq, d = x.shape
    depth = norm1.shape[0]
    bias = _bias_tables(rel_bias)
    x2d = x.reshape(bsz * seq, d)
    for l in range(depth):
        z = _inproj(x2d, norm1[l], w_in[l], act_col=ATTN_W + 2 * KV_W)
        mix = _mix(z.reshape(bsz, seq, -1), bias, attn_sink[l], q_gain[l], k_gain[l],
                   gmlp_v_gain[l], gmlp_w_s[l].astype(BF16), gmlp_b_s[l].T)
        mix_gain = jnp.concatenate([attn_out_gain[l], gmlp_out_gain[l]])
        x1, x1g, inv2 = _outproj(mix.reshape(bsz * seq, -1), w_out[l], mix_gain, x2d, norm2[l])
        hid = _up(x1g, w1[l], inv2)
        x2d = _down(hid, w2[l], x1)
    return x2d.reshape(bsz, seq, d)
```

```python
import functools
import math

import jax
import jax.numpy as jnp
import numpy as np
from jax import lax
from jax.experimental import pallas as pl
from jax.experimental.pallas import tpu as pltpu

HEAD_DIM = 128
N_Q_HEADS = 16
N_KV_HEADS = 4
GQA_GROUP = N_Q_HEADS // N_KV_HEADS
ATTN_W = N_Q_HEADS * HEAD_DIM
KV_W = N_KV_HEADS * HEAD_DIM
WINDOW = 128
BLOCK = 128
N_BUCKETS = 32
MAX_DISTANCE = 128
GMLP_HEADS = 16
GMLP_W = GMLP_HEADS * HEAD_DIM
EPS = 1e-6
NEG = -1e30
LOG2E = math.log2(math.e)

F32 = jnp.float32
BF16 = jnp.bfloat16

V7X_VMEM_LIMIT_BYTES = 60 * 1024 * 1024


def _params(sem):
    return pltpu.CompilerParams(dimension_semantics=sem,
                                vmem_limit_bytes=V7X_VMEM_LIMIT_BYTES)


def _prep_kernel(x_ref, g_ref, xg_ref, inv_ref):
    x = x_ref[...]
    d = x.shape[-1]
    ss = jnp.sum(x * x, axis=-1, keepdims=True)
    inv_ref[...] = lax.rsqrt(ss * (1.0 / d) + EPS)
    xg_ref[...] = (x * g_ref[...]).astype(BF16)


def _prep(x2d, gain, n_rows, rows=256):
    m, d = n_rows, x2d.shape[1]
    return pl.pallas_call(
        _prep_kernel,
        grid=(m // rows,),
        in_specs=[pl.BlockSpec((rows, d), lambda i: (i, 0)),
                  pl.BlockSpec((1, d), lambda i: (0, 0))],
        out_specs=[pl.BlockSpec((rows, d), lambda i: (i, 0)),
                   pl.BlockSpec((rows, 1), lambda i: (i, 0))],
        out_shape=[jax.ShapeDtypeStruct((m, d), BF16),
                   jax.ShapeDtypeStruct((m, 1), F32)],
        compiler_params=_params(("arbitrary",)),
        name="prep_norm",
    )(x2d, gain.reshape(1, d))


FIRST_TN = 512


def _proj_kernel(a_ref, w_ref, *refs, body, n_in, n_prev, n_out, cast_w, scale_w, act_from):
    if scale_w:
        ws_ref, refs = refs[0], refs[1:]
    ins = refs[:n_in]
    outs = refs[n_in + n_prev:n_in + n_prev + n_out]
    rest = refs[n_in + n_prev + n_out:]
    if cast_w:
        wb_ref, rest = rest[0], rest[1:]

    def run(**body_kw):
        if cast_w:
            w = w_ref[...]
            if scale_w:
                w = w * ws_ref[...]
            w = w.astype(BF16)
            wb_ref[...] = w
        else:
            w = w_ref[...]
        acc = jnp.dot(a_ref[...], w, preferred_element_type=F32)
        body(acc, ins, outs, rest, **body_kw)

    if act_from is None:
        run()
    else:
        j = pl.program_id(1)
        pl.when(j < act_from)(functools.partial(run, act=False))
        pl.when(j >= act_from)(functools.partial(run, act=True))


def _project(body, name, a, w, ins, outs, *, tm, tn, prev=None, scratch=(), w_scale=None,
             act_col=None):
    m, k = a.shape
    n = w.shape[1]
    first = prev is None
    ioff = 0 if first else 1
    ni = 1 if first else m // tm - 1

    def spec(kind):
        if kind == "row":
            return pl.BlockSpec((tm, 1), lambda i, j: (i + ioff, 0))
        if kind == "col":
            return pl.BlockSpec((1, tn), lambda i, j: (0, j))
        return pl.BlockSpec((tm, tn), lambda i, j: (i + ioff, j))

    a_mode = dict(pipeline_mode=pl.Buffered(1)) if first else {}
    in_specs = [pl.BlockSpec((tm, k), lambda i, j: (i + ioff, 0), **a_mode),
                pl.BlockSpec((k, tn), lambda i, j: (0, j))]
    args = [a, w]
    scale_w = first and w_scale is not None
    if scale_w:
        in_specs.append(pl.BlockSpec((k, 1), lambda i, j: (0, 0), pipeline_mode=pl.Buffered(1)))
        args.append(w_scale)
    in_specs += [spec(kind) for _, kind in ins]
    args += [arr for arr, _ in ins]
    out_specs = [spec(kind) for _, kind in outs]
    out_shape = [sds for sds, _ in outs]
    aliases = {}
    if first:
        out_specs.append(pl.BlockSpec((k, tn), lambda i, j: (0, j)))
        out_shape.append(jax.ShapeDtypeStruct((k, n), BF16))
    else:
        for idx, p in enumerate(prev):
            aliases[len(args)] = idx
            in_specs.append(pl.BlockSpec(memory_space=pl.ANY))
            args.append(p)
    assert act_col is None or act_col % tn == 0
    return pl.pallas_call(
        functools.partial(_proj_kernel, body=body, n_in=len(ins), n_prev=len(aliases),
                          n_out=len(outs), cast_w=first, scale_w=scale_w,
                          act_from=None if act_col is None else act_col // tn),
        grid=(ni, n // tn),
        in_specs=in_specs,
        out_specs=out_specs,
        out_shape=out_shape,
        input_output_aliases=aliases,
        scratch_shapes=list(scratch),
        compiler_params=_params(("arbitrary", "arbitrary")),
        name=name + ("_first" if first else "_rest"),
    )(*args)


def _two_pass(body, name, a, w_f32, ins, outs, *, tm, tn, scratch=(), w_scale=None):
    *part, wb = _project(body, name, a, w_f32, ins, outs, tm=tm, tn=FIRST_TN, scratch=scratch,
                         w_scale=w_scale)
    return _project(body, name, a, wb, ins, outs, tm=tm, tn=tn, prev=part, scratch=scratch)


def _gelu_tanh(x):
    c = math.sqrt(2.0 / math.pi)
    inner = x * (c + (c * 0.044715) * (x * x))
    return (0.5 * x) * (1.0 + jnp.tanh(inner))


def _scale_body(acc, ins, outs, scratch, act=False):
    (inv_ref,), (o_ref,) = ins, outs
    t = acc * inv_ref[...]
    o_ref[...] = (_gelu_tanh(t) if act else t).astype(o_ref.dtype)


def _relu2_body(acc, ins, outs, scratch):
    (inv_ref,), (o_ref,) = ins, outs
    r = jnp.maximum(acc * inv_ref[...], 0.0)
    o_ref[...] = (r * r).astype(o_ref.dtype)


def _residual_norm_body(acc, ins, outs, scratch, *, d_model):
    (x_ref, g_ref), (x1_ref, x1g_ref, inv_ref), (ss_ref,) = ins, outs, scratch
    j = pl.program_id(1)
    x1 = x_ref[...] + acc
    x1_ref[...] = x1
    x1g_ref[...] = (x1 * g_ref[...]).astype(BF16)
    part = jnp.sum(x1 * x1, axis=-1, keepdims=True)

    @pl.when(j == 0)
    def _():
        ss_ref[...] = part

    @pl.when(j > 0)
    def _():
        ss_ref[...] += part

    @pl.when(j == pl.num_programs(1) - 1)
    def _():
        inv_ref[...] = lax.rsqrt(ss_ref[...] * (1.0 / d_model) + EPS)


PREP_ROWS = 256


def _inproj_rest_kernel(x_ref, g_ref, w_ref, z_prev_ref, o_ref, xg_scr, inv_scr,
                        *, n_row_blocks, prep_steps, act_from):
    del z_prev_ref
    r = pl.program_id(0)
    j = pl.program_id(1)
    slot_in, slot_out = lax.rem(r, 2), lax.rem(r + 1, 2)

    def project(act):
        acc = jnp.dot(xg_scr[slot_in], w_ref[...], preferred_element_type=F32)
        t = acc * inv_scr[slot_in]
        o_ref[...] = (_gelu_tanh(t) if act else t).astype(o_ref.dtype)

    def prep():
        x = x_ref[...]
        rows = pl.ds(pl.multiple_of(j * PREP_ROWS, PREP_ROWS), PREP_ROWS)
        ss = jnp.sum(x * x, axis=-1, keepdims=True)
        inv_scr[slot_out, rows, :] = lax.rsqrt(ss * (1.0 / x.shape[-1]) + EPS)
        xg_scr[slot_out, rows, :] = (x * g_ref[...]).astype(BF16)

    has_next = r < n_row_blocks - 1
    prepping = has_next & (j < prep_steps)

    @pl.when((r == 0) & prepping)
    def _():
        prep()

    for act in (False, True):
        variant = (r > 0) & ((j >= act_from) == act)

        @pl.when(variant & prepping)
        def _():
            project(act)
            prep()

        @pl.when(variant & jnp.logical_not(prepping))
        def _():
            project(act)


def _inproj(x2d, gain, w_in, act_col, tm=1024, tn=1024):
    m, k = x2d.shape
    n = w_in.shape[1]
    ni, nj = m // tm, n // tn
    prep_steps = tm // PREP_ROWS
    assert prep_steps <= nj and act_col % tn == 0
    xg0, inv0 = _prep(x2d, gain, tm)
    z_part, wb = _project(_scale_body, "in_proj", xg0, w_in, [(inv0, "row")],
                          [(jax.ShapeDtypeStruct((m, n), BF16), "tile")], tm=tm, tn=FIRST_TN,
                          act_col=act_col)

    def x_map(r, j):
        return (jnp.minimum(r + 1, ni - 1) * prep_steps + jnp.minimum(j, prep_steps - 1), 0)

    def col(r, j):
        return jnp.where(r == 0, 0, j)

    return pl.pallas_call(
        functools.partial(_inproj_rest_kernel, n_row_blocks=ni, prep_steps=prep_steps,
                          act_from=act_col // tn),
        grid=(ni, nj),
        in_specs=[pl.BlockSpec((PREP_ROWS, k), x_map),
                  pl.BlockSpec((1, k), lambda r, j: (0, 0)),
                  pl.BlockSpec((k, tn), lambda r, j: (0, col(r, j))),
                  pl.BlockSpec(memory_space=pl.ANY)],
        out_specs=pl.BlockSpec((tm, tn), lambda r, j: (jnp.maximum(r, 1), col(r, j))),
        out_shape=jax.ShapeDtypeStruct((m, n), BF16),
        input_output_aliases={3: 0},
        scratch_shapes=[pltpu.VMEM((2, tm, k), BF16), pltpu.VMEM((2, tm, 1), F32)],
        compiler_params=_params(("arbitrary", "arbitrary")),
        name="in_proj_rest",
    )(x2d, gain.reshape(1, k), wb, z_part)


def _up(x1g, w1, inv, tm=1024, tn=1024):
    m, n = x1g.shape[0], w1.shape[1]
    (hid,) = _two_pass(_relu2_body, "mlp_up", x1g, w1, [(inv, "row")],
                       [(jax.ShapeDtypeStruct((m, n), BF16), "tile")], tm=tm, tn=tn)
    return hid


def _outproj(mix, w_out, mix_gain, x2d, gain, tm=1024, tn=512):
    m, n = x2d.shape
    outs = [(jax.ShapeDtypeStruct((m, n), F32), "tile"),
            (jax.ShapeDtypeStruct((m, n), BF16), "tile"),
            (jax.ShapeDtypeStruct((m, 1), F32), "row")]
    return _two_pass(functools.partial(_residual_norm_body, d_model=n), "out_proj", mix, w_out,
                     [(x2d, "tile"), (gain.reshape(1, n), "col")], outs, tm=tm, tn=tn,
                     scratch=[pltpu.VMEM((tm, 1), F32)], w_scale=mix_gain.reshape(-1, 1))


def _down_kernel(a_ref, w_ref, x1_ref, *refs, cast_w):
    o_ref = refs[0] if cast_w else refs[1]

    @pl.when(pl.program_id(2) == 0)
    def _():
        o_ref[...] = x1_ref[...]

    if cast_w:
        wb_ref = refs[1]
        w = w_ref[...].astype(BF16)
        wb_ref[...] = w
    else:
        w = w_ref[...]
    o_ref[...] += jnp.dot(a_ref[...], w, preferred_element_type=F32)


def _down_call(a, w, x1, *, tm, tn, tk, prev=None):
    m, k = a.shape
    n = w.shape[1]
    first = prev is None
    ioff = 0 if first else 1
    ni = 1 if first else m // tm - 1
    in_specs = [pl.BlockSpec((tm, tk), lambda i, j, kk: (i + ioff, kk)),
                pl.BlockSpec((tk, tn), lambda i, j, kk: (kk, j)),
                pl.BlockSpec((tm, tn), lambda i, j, kk: (i + ioff, j))]
    args = [a, w, x1]
    out_specs = [pl.BlockSpec((tm, tn), lambda i, j, kk: (i + ioff, j))]
    out_shape = [jax.ShapeDtypeStruct((m, n), F32)]
    aliases = {}
    if first:
        out_specs.append(pl.BlockSpec((tk, tn), lambda i, j, kk: (kk, j)))
        out_shape.append(jax.ShapeDtypeStruct((k, n), BF16))
    else:
        aliases[len(args)] = 0
        in_specs.append(pl.BlockSpec(memory_space=pl.ANY))
        args.append(prev)
    return pl.pallas_call(
        functools.partial(_down_kernel, cast_w=first),
        grid=(ni, n // tn, k // tk),
        in_specs=in_specs,
        out_specs=out_specs,
        out_shape=out_shape,
        input_output_aliases=aliases,
        compiler_params=_params(("arbitrary", "arbitrary", "arbitrary")),
        name="mlp_down" + ("_first" if first else "_rest"),
    )(*args)


def _down(hid, w2, x1, tm=1024, tn=1024, tk=4096):
    part, wb = _down_call(hid, w2, x1, tm=tm, tn=tn, tk=tk // 2)
    (y,) = _down_call(hid, wb, x1, tm=tm, tn=tn, tk=tk, prev=part)
    return y


def _t5_bucket(rel):
    nb = N_BUCKETS // 2
    max_exact = nb // 2
    ret = (rel > 0).astype(np.int32) * nb
    n = np.abs(rel)
    large = max_exact + (np.log(np.maximum(n, 1).astype(np.float32) / max_exact)
                         / math.log(MAX_DISTANCE / max_exact) * (nb - max_exact)).astype(np.int32)
    large = np.minimum(large, nb - 1)
    return ret + np.where(n < max_exact, n, large)


def _bucket_table():
    a = np.arange(BLOCK)[:, None]
    s = np.arange(3 * BLOCK)[None, :]
    rel = s - BLOCK - a
    return np.where(np.abs(rel) <= WINDOW, _t5_bucket(rel), -1).astype(np.int32)


def _bias_kernel(rb_ref, bk_ref, o_ref):
    h = pl.program_id(0)
    bk = bk_ref[...]
    acc = jnp.full(bk.shape, NEG, F32)
    for b in range(N_BUCKETS):
        acc = jnp.where(bk == b, rb_ref[b, h] * LOG2E, acc)
    col = lax.broadcasted_iota(jnp.int32, bk.shape, 1)
    o_ref[0] = jnp.where(col >= BLOCK, acc, NEG)
    o_ref[1] = acc
    o_ref[2] = jnp.where(col < 2 * BLOCK, acc, NEG)


def _bias_tables(rel_bias):
    return pl.pallas_call(
        _bias_kernel,
        grid=(N_Q_HEADS,),
        in_specs=[pl.BlockSpec(memory_space=pltpu.SMEM),
                  pl.BlockSpec((BLOCK, 3 * BLOCK), lambda h: (0, 0))],
        out_specs=pl.BlockSpec((3, None, BLOCK, 3 * BLOCK), lambda h: (0, h, 0, 0)),
        out_shape=jax.ShapeDtypeStruct((3, N_Q_HEADS, BLOCK, 3 * BLOCK), F32),
        compiler_params=_params(("arbitrary",)),
        name="bias_tables",
    )(rel_bias, jnp.asarray(_bucket_table()))


def _rms_scale(t):
    return lax.rsqrt(jnp.sum(t * t, axis=-1, keepdims=True) * (1.0 / t.shape[-1]) + EPS)


MIX_BLOCKS = 4
Z_COL_BLOCK = 1024


def _mix_kernel(sink_ref, q_ref, kp_ref, ko_ref, kn_ref, vp_ref, vo_ref, vn_ref,
                u0_ref, u1_ref, g0_ref, g1_ref, bias_ref, qg_ref, kg_ref,
                vg_ref, ws_ref, bt_ref, mix_ref, a_scr, g_scr):
    n = pl.program_id(1)
    nlast = pl.num_programs(1) - 1
    kfold = kg_ref[...] * qg_ref[...] * (HEAD_DIM ** -0.5 * LOG2E)

    kband = jnp.concatenate([kp_ref[...], ko_ref[...], kn_ref[...]], axis=0).astype(F32)
    vband = jnp.concatenate([vp_ref[...], vo_ref[...], vn_ref[...]], axis=0)
    ones = jnp.ones((3 * BLOCK, HEAD_DIM), BF16)
    for kh in range(N_KV_HEADS):
        k = kband[:, kh * HEAD_DIM:(kh + 1) * HEAD_DIM]
        kn_all = (k * _rms_scale(k) * kfold).astype(BF16)
        for sb in range(MIX_BLOCKS):
            tok = slice(sb * BLOCK, (sb + 1) * BLOCK)
            band = slice(sb * BLOCK, (sb + 3) * BLOCK)
            if sb == 0:
                variant = jnp.where(n == 0, 0, 1)
            elif sb == MIX_BLOCKS - 1:
                variant = jnp.where(n == nlast, 2, 1)
            else:
                variant = 1
            kn = kn_all[band]
            v1 = jnp.concatenate([vband[band, kh * HEAD_DIM:(kh + 1) * HEAD_DIM], ones], axis=1)
            qs = []
            for g in range(GQA_GROUP):
                h = kh * GQA_GROUP + g
                qh = q_ref[tok, h * HEAD_DIM:(h + 1) * HEAD_DIM].astype(F32)
                qs.append((qh * _rms_scale(qh)).astype(BF16))
            qstack = jnp.concatenate(qs, axis=0)
            s = lax.dot_general(qstack, kn, (((1,), (1,)), ((), ())),
                                preferred_element_type=F32)
            ps, sink_terms = [], []
            for g in range(GQA_GROUP):
                h = kh * GQA_GROUP + g
                sink = sink_ref[h] * LOG2E
                sg = s[g * BLOCK:(g + 1) * BLOCK] + bias_ref[variant, h]
                m = jnp.maximum(jnp.max(sg, axis=-1, keepdims=True), sink)
                ps.append(jnp.exp2(sg - m).astype(BF16))
                sink_terms.append(jnp.exp2(sink - m))
            pstack = jnp.concatenate(ps, axis=0)
            ol = jnp.dot(pstack, v1, preferred_element_type=F32)
            for g in range(GQA_GROUP):
                h = kh * GQA_GROUP + g
                rows = slice(g * BLOCK, (g + 1) * BLOCK)
                denom = ol[rows, HEAD_DIM:] + sink_terms[g]
                a_scr[tok, h * HEAD_DIM:(h + 1) * HEAD_DIM] = ol[rows, :HEAD_DIM] * (1.0 / denom)

    a = a_scr[...]
    mix_ref[:, :ATTN_W] = (a * _rms_scale(a)).astype(mix_ref.dtype)

    gu = jnp.concatenate([u0_ref[...], u1_ref[...]], axis=1).astype(F32)
    gv = jnp.concatenate([g0_ref[...], g1_ref[...]], axis=1).astype(F32)
    vn_ = (gv * _rms_scale(gv) * vg_ref[...]).astype(BF16)
    for sb in range(MIX_BLOCKS):
        tok = slice(sb * BLOCK, (sb + 1) * BLOCK)
        for h in range(GMLP_HEADS):
            sl = slice(h * HEAD_DIM, (h + 1) * HEAD_DIM)
            sv = jnp.dot(ws_ref[h], vn_[tok, sl], preferred_element_type=F32) + bt_ref[:, h:h + 1]
            g_scr[tok, sl] = gu[tok, sl] * sv
    gg = g_scr[...]
    mix_ref[:, ATTN_W:] = (gg * _rms_scale(gg)).astype(mix_ref.dtype)


def _mix(z3, bias, sink, q_gain, k_gain, v_gain, w_s, b_t):
    bsz, seq, _ = z3.shape
    nblk = seq // BLOCK
    rows = MIX_BLOCKS * BLOCK
    assert nblk >= 2 and nblk % MIX_BLOCKS == 0
    kcol = ATTN_W // KV_W
    vcol = kcol + 1
    ucol = (ATTN_W + 2 * KV_W) // Z_COL_BLOCK
    gcol = ucol + GMLP_W // Z_COL_BLOCK
    assert (ATTN_W + 2 * KV_W) % Z_COL_BLOCK == 0 and GMLP_W == 2 * Z_COL_BLOCK

    def own(width, col):
        return pl.BlockSpec((None, rows, width), lambda b, n: (b, n, col))

    def edge(width, col, after):
        def imap(b, n):
            blk = n * MIX_BLOCKS + (MIX_BLOCKS if after else -1)
            return (b, jnp.clip(blk, 0, nblk - 1), col)
        return pl.BlockSpec((None, BLOCK, width), imap)

    def full(shape):
        return pl.BlockSpec(shape, lambda b, n: (0,) * len(shape))

    in_specs = [
        pl.BlockSpec(memory_space=pltpu.SMEM),
        own(ATTN_W, 0),
        edge(KV_W, kcol, False), own(KV_W, kcol), edge(KV_W, kcol, True),
        edge(KV_W, vcol, False), own(KV_W, vcol), edge(KV_W, vcol, True),
        own(Z_COL_BLOCK, ucol), own(Z_COL_BLOCK, ucol + 1),
        own(Z_COL_BLOCK, gcol), own(Z_COL_BLOCK, gcol + 1),
        pl.BlockSpec(bias.shape, lambda b, n: (0, 0, 0, 0), pipeline_mode=pl.Buffered(1)),
        full((1, HEAD_DIM)), full((1, HEAD_DIM)),
        full((1, GMLP_W)), full(w_s.shape), full(b_t.shape),
    ]
    return pl.pallas_call(
        _mix_kernel,
        grid=(bsz, nblk // MIX_BLOCKS),
        in_specs=in_specs,
        out_specs=pl.BlockSpec((None, rows, ATTN_W + GMLP_W), lambda b, n: (b, n, 0)),
        out_shape=jax.ShapeDtypeStruct((bsz, seq, ATTN_W + GMLP_W), BF16),
        scratch_shapes=[pltpu.VMEM((rows, ATTN_W), F32), pltpu.VMEM((rows, GMLP_W), F32)],
        compiler_params=_params(("arbitrary", "arbitrary")),
        name="mix",
    )(sink, z3, z3, z3, z3, z3, z3, z3, z3, z3, z3, z3, bias,
      q_gain.reshape(1, HEAD_DIM), k_gain.reshape(1, HEAD_DIM),
      v_gain.reshape(1, GMLP_W), w_s, b_t)


def kernel(x, norm1, w_in, q_gain, k_gain, rel_bias, attn_sink, attn_out_gain,
           gmlp_v_gain, gmlp_w_s, gmlp_b_s, gmlp_out_gain, w_out, norm2, w1, w2):
    bsz, seq, d = x.shape
    depth = norm1.shape[0]
    bias = _bias_tables(rel_bias)
    x2d = x.reshape(bsz * seq, d)
    for l in range(depth):
        z = _inproj(x2d, norm1[l], w_in[l], act_col=ATTN_W + 2 * KV_W)
        mix = _mix(z.reshape(bsz, seq, -1), bias, attn_sink[l], q_gain[l], k_gain[l],
                   gmlp_v_gain[l], gmlp_w_s[l].astype(BF16), gmlp_b_s[l].T)
        mix_gain = jnp.concatenate([attn_out_gain[l], gmlp_out_gain[l]])
        x1, x1g, inv2 = _outproj(mix.reshape(bsz * seq, -1), w_out[l], mix_gain, x2d, norm2[l])
        hid = _up(x1g, w1[l], inv2)
        x2d = _down(hid, w2[l], x1)
    return x2d.reshape(bsz, seq, d)
```

```python
import functools
import math

import jax
import jax.numpy as jnp
import numpy as np
from jax import lax
from jax.experimental import pallas as pl
from jax.experimental.pallas import tpu as pltpu

HEAD_DIM = 128
N_Q_HEADS = 16
N_KV_HEADS = 4
GQA_GROUP = N_Q_HEADS // N_KV_HEADS
ATTN_W = N_Q_HEADS * HEAD_DIM
KV_W = N_KV_HEADS * HEAD_DIM
WINDOW = 128
BLOCK = 128
N_BUCKETS = 32
MAX_DISTANCE = 128
GMLP_HEADS = 16
GMLP_W = GMLP_HEADS * HEAD_DIM
EPS = 1e-6
NEG = -1e30
LOG2E = math.log2(math.e)

F32 = jnp.float32
BF16 = jnp.bfloat16

V7X_VMEM_LIMIT_BYTES = 60 * 1024 * 1024


def _params(sem):
    return pltpu.CompilerParams(dimension_semantics=sem,
                                vmem_limit_bytes=V7X_VMEM_LIMIT_BYTES)


def _prep_kernel(x_ref, g_ref, xg_ref, inv_ref):
    x = x_ref[...]
    d = x.shape[-1]
    ss = jnp.sum(x * x, axis=-1, keepdims=True)
    inv_ref[...] = lax.rsqrt(ss * (1.0 / d) + EPS)
    xg_ref[...] = (x * g_ref[...]).astype(BF16)


def _prep(x2d, gain, n_rows, rows=256):
    m, d = n_rows, x2d.shape[1]
    return pl.pallas_call(
        _prep_kernel,
        grid=(m // rows,),
        in_specs=[pl.BlockSpec((rows, d), lambda i: (i, 0)),
                  pl.BlockSpec((1, d), lambda i: (0, 0))],
        out_specs=[pl.BlockSpec((rows, d), lambda i: (i, 0)),
                   pl.BlockSpec((rows, 1), lambda i: (i, 0))],
        out_shape=[jax.ShapeDtypeStruct((m, d), BF16),
                   jax.ShapeDtypeStruct((m, 1), F32)],
        compiler_params=_params(("arbitrary",)),
        name="prep_norm",
    )(x2d, gain.reshape(1, d))


FIRST_TN = 512


def _cast_chunk(step, n_chunks, src_ref, scale_ref, dst_ref):
    @pl.when(step < n_chunks)
    def _():
        w = src_ref[...]
        if scale_ref is not None:
            w = w * scale_ref[...]
        dst_ref[...] = w.astype(BF16)


def _proj_kernel(a_ref, w_ref, *refs, body, n_in, n_prev, n_out, cast_w, scale_w, act_from, side):
    if scale_w:
        ws_ref, refs = refs[0], refs[1:]
    ins = refs[:n_in]
    refs = refs[n_in:]
    if side is not None:
        side_src, refs = refs[0], refs[1:]
        side_scale = None
        if side[1]:
            side_scale, refs = refs[0], refs[1:]
    outs = refs[n_prev:n_prev + n_out]
    rest = refs[n_prev + n_out:]
    if cast_w:
        wb_ref, rest = rest[0], rest[1:]
    if side is not None:
        side_dst, rest = rest[0], rest[1:]

    def run(**body_kw):
        if cast_w:
            w = w_ref[...]
            if scale_w:
                w = w * ws_ref[...]
            w = w.astype(BF16)
            wb_ref[...] = w
        else:
            w = w_ref[...]
        acc = jnp.dot(a_ref[...], w, preferred_element_type=F32)
        body(acc, ins, outs, rest, **body_kw)

    if act_from is None:
        run()
    else:
        j = pl.program_id(1)
        pl.when(j < act_from)(functools.partial(run, act=False))
        pl.when(j >= act_from)(functools.partial(run, act=True))
    if side is not None:
        step = pl.program_id(0) * pl.num_programs(1) + pl.program_id(1)
        _cast_chunk(step, side[0], side_src, side_scale, side_dst)


SIDE_CHUNKS = 128


def _side_specs(w_next, scale_next, n_steps, step_of):
    k2, n2 = w_next.shape
    n_chunks = min(SIDE_CHUNKS, 1 << (n_steps.bit_length() - 1))
    rows = k2 // n_chunks
    assert rows * n_chunks == k2 and rows % 16 == 0

    def imap(*ids):
        return (jnp.minimum(step_of(*ids), n_chunks - 1), 0)

    in_specs, args = [pl.BlockSpec((rows, n2), imap)], [w_next]
    if scale_next is not None:
        in_specs.append(pl.BlockSpec((rows, 1), imap))
        args.append(scale_next.reshape(k2, 1))
    return n_chunks, in_specs, args, pl.BlockSpec((rows, n2), imap), jax.ShapeDtypeStruct((k2, n2), BF16)


def _project(body, name, a, w, ins, outs, *, tm, tn, mode, prev=None, scratch=(), w_scale=None,
             act_col=None, side=None):
    m, k = a.shape
    n = w.shape[1]
    first = mode == "first"
    ioff = 1 if mode == "rest" else 0
    ni = {"first": 1, "rest": m // tm - 1, "all": m // tm}[mode]
    nj = n // tn

    def spec(kind):
        if kind == "row":
            return pl.BlockSpec((tm, 1), lambda i, j: (i + ioff, 0))
        if kind == "col":
            return pl.BlockSpec((1, tn), lambda i, j: (0, j))
        return pl.BlockSpec((tm, tn), lambda i, j: (i + ioff, j))

    a_mode = dict(pipeline_mode=pl.Buffered(1)) if first else {}
    in_specs = [pl.BlockSpec((tm, k), lambda i, j: (i + ioff, 0), **a_mode),
                pl.BlockSpec((k, tn), lambda i, j: (0, j))]
    args = [a, w]
    scale_w = first and w_scale is not None
    if scale_w:
        in_specs.append(pl.BlockSpec((k, 1), lambda i, j: (0, 0), pipeline_mode=pl.Buffered(1)))
        args.append(w_scale)
    in_specs += [spec(kind) for _, kind in ins]
    args += [arr for arr, _ in ins]
    side_key = None
    if side is not None:
        n_chunks, s_in, s_args, s_out_spec, s_out_shape = _side_specs(
            side[0], side[1], ni * nj, lambda i, j: i * nj + j)
        in_specs += s_in
        args += s_args
        side_key = (n_chunks, side[1] is not None)
    out_specs = [spec(kind) for _, kind in outs]
    out_shape = [sds for sds, _ in outs]
    aliases = {}
    if first:
        out_specs.append(pl.BlockSpec((k, tn), lambda i, j: (0, j)))
        out_shape.append(jax.ShapeDtypeStruct((k, n), BF16))
    elif mode == "rest":
        for idx, p in enumerate(prev):
            aliases[len(args)] = idx
            in_specs.append(pl.BlockSpec(memory_space=pl.ANY))
            args.append(p)
    if side is not None:
        out_specs.append(s_out_spec)
        out_shape.append(s_out_shape)
    assert act_col is None or act_col % tn == 0
    return pl.pallas_call(
        functools.partial(_proj_kernel, body=body, n_in=len(ins), n_prev=len(aliases),
                          n_out=len(outs), cast_w=first, scale_w=scale_w,
                          act_from=None if act_col is None else act_col // tn, side=side_key),
        grid=(ni, nj),
        in_specs=in_specs,
        out_specs=out_specs,
        out_shape=out_shape,
        input_output_aliases=aliases,
        scratch_shapes=list(scratch),
        compiler_params=_params(("arbitrary", "arbitrary")),
        name=name + {"first": "_first", "rest": "_rest", "all": ""}[mode],
    )(*args)


def _gelu_tanh(x):
    c = math.sqrt(2.0 / math.pi)
    inner = x * (c + (c * 0.044715) * (x * x))
    return (0.5 * x) * (1.0 + jnp.tanh(inner))


def _scale_body(acc, ins, outs, scratch, act=False):
    (inv_ref,), (o_ref,) = ins, outs
    t = acc * inv_ref[...]
    o_ref[...] = (_gelu_tanh(t) if act else t).astype(o_ref.dtype)


def _relu2_body(acc, ins, outs, scratch):
    (inv_ref,), (o_ref,) = ins, outs
    r = jnp.maximum(acc * inv_ref[...], 0.0)
    o_ref[...] = (r * r).astype(o_ref.dtype)


def _residual_norm_body(acc, ins, outs, scratch, *, d_model):
    (x_ref, g_ref), (x1_ref, x1g_ref, inv_ref), (ss_ref,) = ins, outs, scratch
    j = pl.program_id(1)
    x1 = x_ref[...] + acc
    x1_ref[...] = x1
    x1g_ref[...] = (x1 * g_ref[...]).astype(BF16)
    part = jnp.sum(x1 * x1, axis=-1, keepdims=True)

    @pl.when(j == 0)
    def _():
        ss_ref[...] = part

    @pl.when(j > 0)
    def _():
        ss_ref[...] += part

    @pl.when(j == pl.num_programs(1) - 1)
    def _():
        inv_ref[...] = lax.rsqrt(ss_ref[...] * (1.0 / d_model) + EPS)


PREP_ROWS = 256


def _inproj_rest_kernel(x_ref, g_ref, w_ref, z_prev_ref, o_ref, xg_scr, inv_scr,
                        *, n_row_blocks, prep_steps, act_from):
    del z_prev_ref
    r = pl.program_id(0)
    j = pl.program_id(1)
    slot_in, slot_out = lax.rem(r, 2), lax.rem(r + 1, 2)

    def project(act):
        acc = jnp.dot(xg_scr[slot_in], w_ref[...], preferred_element_type=F32)
        t = acc * inv_scr[slot_in]
        o_ref[...] = (_gelu_tanh(t) if act else t).astype(o_ref.dtype)

    def prep():
        x = x_ref[...]
        rows = pl.ds(pl.multiple_of(j * PREP_ROWS, PREP_ROWS), PREP_ROWS)
        ss = jnp.sum(x * x, axis=-1, keepdims=True)
        inv_scr[slot_out, rows, :] = lax.rsqrt(ss * (1.0 / x.shape[-1]) + EPS)
        xg_scr[slot_out, rows, :] = (x * g_ref[...]).astype(BF16)

    has_next = r < n_row_blocks - 1
    prepping = has_next & (j < prep_steps)

    @pl.when((r == 0) & prepping)
    def _():
        prep()

    for act in (False, True):
        variant = (r > 0) & ((j >= act_from) == act)

        @pl.when(variant & prepping)
        def _():
            project(act)
            prep()

        @pl.when(variant & jnp.logical_not(prepping))
        def _():
            project(act)


def _inproj(x2d, gain, w_in, act_col, tm=1024, tn=1024):
    m, k = x2d.shape
    n = w_in.shape[1]
    ni, nj = m // tm, n // tn
    prep_steps = tm // PREP_ROWS
    assert prep_steps <= nj and act_col % tn == 0
    xg0, inv0 = _prep(x2d, gain, tm)
    z_part, wb = _project(_scale_body, "in_proj", xg0, w_in, [(inv0, "row")],
                          [(jax.ShapeDtypeStruct((m, n), BF16), "tile")], tm=tm, tn=FIRST_TN,
                          mode="first", act_col=act_col)

    def x_map(r, j):
        return (jnp.minimum(r + 1, ni - 1) * prep_steps + jnp.minimum(j, prep_steps - 1), 0)

    def col(r, j):
        return jnp.where(r == 0, 0, j)

    return pl.pallas_call(
        functools.partial(_inproj_rest_kernel, n_row_blocks=ni, prep_steps=prep_steps,
                          act_from=act_col // tn),
        grid=(ni, nj),
        in_specs=[pl.BlockSpec((PREP_ROWS, k), x_map),
                  pl.BlockSpec((1, k), lambda r, j: (0, 0)),
                  pl.BlockSpec((k, tn), lambda r, j: (0, col(r, j))),
                  pl.BlockSpec(memory_space=pl.ANY)],
        out_specs=pl.BlockSpec((tm, tn), lambda r, j: (jnp.maximum(r, 1), col(r, j))),
        out_shape=jax.ShapeDtypeStruct((m, n), BF16),
        input_output_aliases={3: 0},
        scratch_shapes=[pltpu.VMEM((2, tm, k), BF16), pltpu.VMEM((2, tm, 1), F32)],
        compiler_params=_params(("arbitrary", "arbitrary")),
        name="in_proj_rest",
    )(x2d, gain.reshape(1, k), wb, z_part)


def _up(x1g, w1b, inv, w_next, tm=1024, tn=1024):
    m, n = x1g.shape[0], w1b.shape[1]
    return _project(_relu2_body, "mlp_up", x1g, w1b, [(inv, "row")],
                    [(jax.ShapeDtypeStruct((m, n), BF16), "tile")], tm=tm, tn=tn, mode="all",
                    side=(w_next, None))


def _outproj(mix, w_outb, x2d, gain, w_next, tm=1024, tn=512):
    m, n = x2d.shape
    outs = [(jax.ShapeDtypeStruct((m, n), F32), "tile"),
            (jax.ShapeDtypeStruct((m, n), BF16), "tile"),
            (jax.ShapeDtypeStruct((m, 1), F32), "row")]
    return _project(functools.partial(_residual_norm_body, d_model=n), "out_proj", mix, w_outb,
                    [(x2d, "tile"), (gain.reshape(1, n), "col")], outs, tm=tm, tn=tn, mode="all",
                    scratch=[pltpu.VMEM((tm, 1), F32)], side=(w_next, None))


def _down_kernel(a_ref, w_ref, x1_ref, o_ref):
    @pl.when(pl.program_id(2) == 0)
    def _():
        o_ref[...] = x1_ref[...]

    o_ref[...] += jnp.dot(a_ref[...], w_ref[...], preferred_element_type=F32)


def _down(hid, w2b, x1, tm=1024, tn=1024, tk=4096):
    m, k = hid.shape
    n = w2b.shape[1]
    return pl.pallas_call(
        _down_kernel,
        grid=(m // tm, n // tn, k // tk),
        in_specs=[pl.BlockSpec((tm, tk), lambda i, j, kk: (i, kk)),
                  pl.BlockSpec((tk, tn), lambda i, j, kk: (kk, j)),
                  pl.BlockSpec((tm, tn), lambda i, j, kk: (i, j))],
        out_specs=pl.BlockSpec((tm, tn), lambda i, j, kk: (i, j)),
        out_shape=jax.ShapeDtypeStruct((m, n), F32),
        compiler_params=_params(("arbitrary", "arbitrary", "arbitrary")),
        name="mlp_down",
    )(hid, w2b, x1)


def _t5_bucket(rel):
    nb = N_BUCKETS // 2
    max_exact = nb // 2
    ret = (rel > 0).astype(np.int32) * nb
    n = np.abs(rel)
    large = max_exact + (np.log(np.maximum(n, 1).astype(np.float32) / max_exact)
                         / math.log(MAX_DISTANCE / max_exact) * (nb - max_exact)).astype(np.int32)
    large = np.minimum(large, nb - 1)
    return ret + np.where(n < max_exact, n, large)


def _bucket_table():
    a = np.arange(BLOCK)[:, None]
    s = np.arange(3 * BLOCK)[None, :]
    rel = s - BLOCK - a
    return np.where(np.abs(rel) <= WINDOW, _t5_bucket(rel), -1).astype(np.int32)


def _bias_kernel(rb_ref, bk_ref, o_ref):
    h = pl.program_id(0)
    bk = bk_ref[...]
    acc = jnp.full(bk.shape, NEG, F32)
    for b in range(N_BUCKETS):
        acc = jnp.where(bk == b, rb_ref[b, h] * LOG2E, acc)
    col = lax.broadcasted_iota(jnp.int32, bk.shape, 1)
    o_ref[0] = jnp.where(col >= BLOCK, acc, NEG)
    o_ref[1] = acc
    o_ref[2] = jnp.where(col < 2 * BLOCK, acc, NEG)


def _bias_tables(rel_bias):
    return pl.pallas_call(
        _bias_kernel,
        grid=(N_Q_HEADS,),
        in_specs=[pl.BlockSpec(memory_space=pltpu.SMEM),
                  pl.BlockSpec((BLOCK, 3 * BLOCK), lambda h: (0, 0))],
        out_specs=pl.BlockSpec((3, None, BLOCK, 3 * BLOCK), lambda h: (0, h, 0, 0)),
        out_shape=jax.ShapeDtypeStruct((3, N_Q_HEADS, BLOCK, 3 * BLOCK), F32),
        compiler_params=_params(("arbitrary",)),
        name="bias_tables",
    )(rel_bias, jnp.asarray(_bucket_table()))


def _rms_scale(t):
    return lax.rsqrt(jnp.sum(t * t, axis=-1, keepdims=True) * (1.0 / t.shape[-1]) + EPS)


MIX_BLOCKS = 2
Z_COL_BLOCK = 1024


def _mix_kernel(sink_ref, q_ref, kp_ref, ko_ref, kn_ref, vp_ref, vo_ref, vn_ref,
                u0_ref, u1_ref, g0_ref, g1_ref, bias_ref, qg_ref, kg_ref,
                vg_ref, ws_ref, bt_ref, side_src, side_scale, mix_ref, side_dst, a_scr, g_scr,
                *, n_chunks):
    n = pl.program_id(1)
    nlast = pl.num_programs(1) - 1
    kfold = kg_ref[...] * qg_ref[...] * (HEAD_DIM ** -0.5 * LOG2E)

    kband = jnp.concatenate([kp_ref[...], ko_ref[...], kn_ref[...]], axis=0).astype(F32)
    vband = jnp.concatenate([vp_ref[...], vo_ref[...], vn_ref[...]], axis=0)
    ones = jnp.ones((3 * BLOCK, HEAD_DIM), BF16)
    for kh in range(N_KV_HEADS):
        k = kband[:, kh * HEAD_DIM:(kh + 1) * HEAD_DIM]
        kn_all = (k * _rms_scale(k) * kfold).astype(BF16)
        for sb in range(MIX_BLOCKS):
            tok = slice(sb * BLOCK, (sb + 1) * BLOCK)
            band = slice(sb * BLOCK, (sb + 3) * BLOCK)
            if sb == 0:
                variant = jnp.where(n == 0, 0, 1)
            elif sb == MIX_BLOCKS - 1:
                variant = jnp.where(n == nlast, 2, 1)
            else:
                variant = 1
            kn = kn_all[band]
            v1 = jnp.concatenate([vband[band, kh * HEAD_DIM:(kh + 1) * HEAD_DIM], ones], axis=1)
            qs = []
            for g in range(GQA_GROUP):
                h = kh * GQA_GROUP + g
                qh = q_ref[tok, h * HEAD_DIM:(h + 1) * HEAD_DIM].astype(F32)
                qs.append((qh * _rms_scale(qh)).astype(BF16))
            qstack = jnp.concatenate(qs, axis=0)
            s = lax.dot_general(qstack, kn, (((1,), (1,)), ((), ())),
                                preferred_element_type=F32)
            ps, sink_terms = [], []
            for g in range(GQA_GROUP):
                h = kh * GQA_GROUP + g
                sink = sink_ref[h] * LOG2E
                sg = s[g * BLOCK:(g + 1) * BLOCK] + bias_ref[variant, h]
                m = jnp.maximum(jnp.max(sg, axis=-1, keepdims=True), sink)
                ps.append(jnp.exp2(sg - m).astype(BF16))
                sink_terms.append(jnp.exp2(sink - m))
            pstack = jnp.concatenate(ps, axis=0)
            ol = jnp.dot(pstack, v1, preferred_element_type=F32)
            for g in range(GQA_GROUP):
                h = kh * GQA_GROUP + g
                rows = slice(g * BLOCK, (g + 1) * BLOCK)
                denom = ol[rows, HEAD_DIM:] + sink_terms[g]
                a_scr[tok, h * HEAD_DIM:(h + 1) * HEAD_DIM] = ol[rows, :HEAD_DIM] * (1.0 / denom)

    a = a_scr[...]
    mix_ref[:, :ATTN_W] = (a * _rms_scale(a)).astype(mix_ref.dtype)

    gu = jnp.concatenate([u0_ref[...], u1_ref[...]], axis=1).astype(F32)
    gv = jnp.concatenate([g0_ref[...], g1_ref[...]], axis=1).astype(F32)
    vn_ = (gv * _rms_scale(gv) * vg_ref[...]).astype(BF16)
    for sb in range(MIX_BLOCKS):
        tok = slice(sb * BLOCK, (sb + 1) * BLOCK)
        for h in range(GMLP_HEADS):
            sl = slice(h * HEAD_DIM, (h + 1) * HEAD_DIM)
            sv = jnp.dot(ws_ref[h], vn_[tok, sl], preferred_element_type=F32) + bt_ref[:, h:h + 1]
            g_scr[tok, sl] = gu[tok, sl] * sv
    gg = g_scr[...]
    mix_ref[:, ATTN_W:] = (gg * _rms_scale(gg)).astype(mix_ref.dtype)
    _cast_chunk(pl.program_id(0) * pl.num_programs(1) + n, n_chunks, side_src, side_scale, side_dst)


def _mix(z3, bias, sink, q_gain, k_gain, v_gain, w_s, b_t, w_next, scale_next):
    bsz, seq, _ = z3.shape
    nblk = seq // BLOCK
    rows = MIX_BLOCKS * BLOCK
    assert nblk >= 2 and nblk % MIX_BLOCKS == 0
    kcol = ATTN_W // KV_W
    vcol = kcol + 1
    ucol = (ATTN_W + 2 * KV_W) // Z_COL_BLOCK
    gcol = ucol + GMLP_W // Z_COL_BLOCK
    assert (ATTN_W + 2 * KV_W) % Z_COL_BLOCK == 0 and GMLP_W == 2 * Z_COL_BLOCK
    n_steps = nblk // MIX_BLOCKS
    n_chunks, s_in, s_args, s_out_spec, s_out_shape = _side_specs(
        w_next, scale_next, bsz * n_steps, lambda b, n: b * n_steps + n)

    def own(width, col):
        return pl.BlockSpec((None, rows, width), lambda b, n: (b, n, col))

    def edge(width, col, after):
        def imap(b, n):
            blk = n * MIX_BLOCKS + (MIX_BLOCKS if after else -1)
            return (b, jnp.clip(blk, 0, nblk - 1), col)
        return pl.BlockSpec((None, BLOCK, width), imap)

    def full(shape):
        return pl.BlockSpec(shape, lambda b, n: (0,) * len(shape))

    in_specs = [
        pl.BlockSpec(memory_space=pltpu.SMEM),
        own(ATTN_W, 0),
        edge(KV_W, kcol, False), own(KV_W, kcol), edge(KV_W, kcol, True),
        edge(KV_W, vcol, False), own(KV_W, vcol), edge(KV_W, vcol, True),
        own(Z_COL_BLOCK, ucol), own(Z_COL_BLOCK, ucol + 1),
        own(Z_COL_BLOCK, gcol), own(Z_COL_BLOCK, gcol + 1),
        pl.BlockSpec(bias.shape, lambda b, n: (0, 0, 0, 0), pipeline_mode=pl.Buffered(1)),
        full((1, HEAD_DIM)), full((1, HEAD_DIM)),
        full((1, GMLP_W)), full(w_s.shape), full(b_t.shape),
    ] + s_in
    return pl.pallas_call(
        functools.partial(_mix_kernel, n_chunks=n_chunks),
        grid=(bsz, n_steps),
        in_specs=in_specs,
        out_specs=[pl.BlockSpec((None, rows, ATTN_W + GMLP_W), lambda b, n: (b, n, 0)), s_out_spec],
        out_shape=[jax.ShapeDtypeStruct((bsz, seq, ATTN_W + GMLP_W), BF16), s_out_shape],
        scratch_shapes=[pltpu.VMEM((rows, ATTN_W), F32), pltpu.VMEM((rows, GMLP_W), F32)],
        compiler_params=_params(("arbitrary", "arbitrary")),
        name="mix",
    )(sink, z3, z3, z3, z3, z3, z3, z3, z3, z3, z3, z3, bias,
      q_gain.reshape(1, HEAD_DIM), k_gain.reshape(1, HEAD_DIM),
      v_gain.reshape(1, GMLP_W), w_s, b_t, *s_args)


def kernel(x, norm1, w_in, q_gain, k_gain, rel_bias, attn_sink, attn_out_gain,
           gmlp_v_gain, gmlp_w_s, gmlp_b_s, gmlp_out_gain, w_out, norm2, w1, w2):
    bsz, seq, d = x.shape
    depth = norm1.shape[0]
    bias = _bias_tables(rel_bias)
    x2d = x.reshape(bsz * seq, d)
    for l in range(depth):
        mix_gain = jnp.concatenate([attn_out_gain[l], gmlp_out_gain[l]])
        z = _inproj(x2d, norm1[l], w_in[l], ATTN_W + 2 * KV_W)
        mix, w_outb = _mix(z.reshape(bsz, seq, -1), bias, attn_sink[l], q_gain[l], k_gain[l],
                           gmlp_v_gain[l], gmlp_w_s[l].astype(BF16), gmlp_b_s[l].T, w_out[l], mix_gain)
        x1, x1g, inv2, w1b = _outproj(mix.reshape(bsz * seq, -1), w_outb, x2d, norm2[l], w1[l])
        hid, w2b = _up(x1g, w1b, inv2, w2[l])
        x2d = _down(hid, w2b, x1)
    return x2d.reshape(bsz, seq, d)
```

```python
import functools
import math

import jax
import jax.numpy as jnp
import numpy as np
from jax import lax
from jax.experimental import pallas as pl
from jax.experimental.pallas import tpu as pltpu

HEAD_DIM = 128
N_Q_HEADS = 16
N_KV_HEADS = 4
GQA_GROUP = N_Q_HEADS // N_KV_HEADS
ATTN_W = N_Q_HEADS * HEAD_DIM
KV_W = N_KV_HEADS * HEAD_DIM
WINDOW = 128
BLOCK = 128
N_BUCKETS = 32
MAX_DISTANCE = 128
GMLP_HEADS = 16
GMLP_W = GMLP_HEADS * HEAD_DIM
EPS = 1e-6
NEG = -1e30
LOG2E = math.log2(math.e)

F32 = jnp.float32
BF16 = jnp.bfloat16

V7X_VMEM_LIMIT_BYTES = 60 * 1024 * 1024


def _params(sem):
    return pltpu.CompilerParams(dimension_semantics=sem,
                                vmem_limit_bytes=V7X_VMEM_LIMIT_BYTES)


def _prep_kernel(x_ref, g_ref, xg_ref, inv_ref):
    x = x_ref[...]
    d = x.shape[-1]
    ss = jnp.sum(x * x, axis=-1, keepdims=True)
    inv_ref[...] = lax.rsqrt(ss * (1.0 / d) + EPS)
    xg_ref[...] = (x * g_ref[...]).astype(BF16)


def _prep(x2d, gain, n_rows, rows=256):
    m, d = n_rows, x2d.shape[1]
    return pl.pallas_call(
        _prep_kernel,
        grid=(m // rows,),
        in_specs=[pl.BlockSpec((rows, d), lambda i: (i, 0)),
                  pl.BlockSpec((1, d), lambda i: (0, 0))],
        out_specs=[pl.BlockSpec((rows, d), lambda i: (i, 0)),
                   pl.BlockSpec((rows, 1), lambda i: (i, 0))],
        out_shape=[jax.ShapeDtypeStruct((m, d), BF16),
                   jax.ShapeDtypeStruct((m, 1), F32)],
        compiler_params=_params(("arbitrary",)),
        name="prep_norm",
    )(x2d, gain.reshape(1, d))


FIRST_TN = 512


def _cast_chunk(step, n_chunks, src_ref, scale_ref, dst_ref):
    @pl.when(step < n_chunks)
    def _():
        w = src_ref[...]
        if scale_ref is not None:
            w = w * scale_ref[...]
        dst_ref[...] = w.astype(BF16)


def _proj_kernel(a_ref, w_ref, *refs, body, n_in, n_prev, n_out, cast_w, scale_w, act_from, side):
    if scale_w:
        ws_ref, refs = refs[0], refs[1:]
    ins = refs[:n_in]
    refs = refs[n_in:]
    if side is not None:
        side_src, refs = refs[0], refs[1:]
        side_scale = None
        if side[1]:
            side_scale, refs = refs[0], refs[1:]
    outs = refs[n_prev:n_prev + n_out]
    rest = refs[n_prev + n_out:]
    if cast_w:
        wb_ref, rest = rest[0], rest[1:]
    if side is not None:
        side_dst, rest = rest[0], rest[1:]

    def run(**body_kw):
        if cast_w:
            w = w_ref[...]
            if scale_w:
                w = w * ws_ref[...]
            w = w.astype(BF16)
            wb_ref[...] = w
        else:
            w = w_ref[...]
        acc = jnp.dot(a_ref[...], w, preferred_element_type=F32)
        body(acc, ins, outs, rest, **body_kw)

    if act_from is None:
        run()
    else:
        j = pl.program_id(1)
        pl.when(j < act_from)(functools.partial(run, act=False))
        pl.when(j >= act_from)(functools.partial(run, act=True))
    if side is not None:
        step = pl.program_id(0) * pl.num_programs(1) + pl.program_id(1)
        _cast_chunk(step, side[0], side_src, side_scale, side_dst)


SIDE_CHUNKS = 128


def _side_specs(w_next, scale_next, n_steps, step_of):
    k2, n2 = w_next.shape
    n_chunks = min(SIDE_CHUNKS, 1 << (n_steps.bit_length() - 1))
    rows = k2 // n_chunks
    assert rows * n_chunks == k2 and rows % 16 == 0

    def imap(*ids):
        return (jnp.minimum(step_of(*ids), n_chunks - 1), 0)

    in_specs, args = [pl.BlockSpec((rows, n2), imap)], [w_next]
    if scale_next is not None:
        in_specs.append(pl.BlockSpec((rows, 1), imap))
        args.append(scale_next.reshape(k2, 1))
    return n_chunks, in_specs, args, pl.BlockSpec((rows, n2), imap), jax.ShapeDtypeStruct((k2, n2), BF16)


def _project(body, name, a, w, ins, outs, *, tm, tn, mode, prev=None, scratch=(), w_scale=None,
             act_col=None, side=None):
    m, k = a.shape
    n = w.shape[1]
    first = mode == "first"
    ioff = 1 if mode == "rest" else 0
    ni = {"first": 1, "rest": m // tm - 1, "all": m // tm}[mode]
    nj = n // tn

    def spec(kind):
        if kind == "row":
            return pl.BlockSpec((tm, 1), lambda i, j: (i + ioff, 0))
        if kind == "col":
            return pl.BlockSpec((1, tn), lambda i, j: (0, j))
        return pl.BlockSpec((tm, tn), lambda i, j: (i + ioff, j))

    a_mode = dict(pipeline_mode=pl.Buffered(1)) if first else {}
    in_specs = [pl.BlockSpec((tm, k), lambda i, j: (i + ioff, 0), **a_mode),
                pl.BlockSpec((k, tn), lambda i, j: (0, j))]
    args = [a, w]
    scale_w = first and w_scale is not None
    if scale_w:
        in_specs.append(pl.BlockSpec((k, 1), lambda i, j: (0, 0), pipeline_mode=pl.Buffered(1)))
        args.append(w_scale)
    in_specs += [spec(kind) for _, kind in ins]
    args += [arr for arr, _ in ins]
    side_key = None
    if side is not None:
        n_chunks, s_in, s_args, s_out_spec, s_out_shape = _side_specs(
            side[0], side[1], ni * nj, lambda i, j: i * nj + j)
        in_specs += s_in
        args += s_args
        side_key = (n_chunks, side[1] is not None)
    out_specs = [spec(kind) for _, kind in outs]
    out_shape = [sds for sds, _ in outs]
    aliases = {}
    if first:
        out_specs.append(pl.BlockSpec((k, tn), lambda i, j: (0, j)))
        out_shape.append(jax.ShapeDtypeStruct((k, n), BF16))
    elif mode == "rest":
        for idx, p in enumerate(prev):
            aliases[len(args)] = idx
            in_specs.append(pl.BlockSpec(memory_space=pl.ANY))
            args.append(p)
    if side is not None:
        out_specs.append(s_out_spec)
        out_shape.append(s_out_shape)
    assert act_col is None or act_col % tn == 0
    return pl.pallas_call(
        functools.partial(_proj_kernel, body=body, n_in=len(ins), n_prev=len(aliases),
                          n_out=len(outs), cast_w=first, scale_w=scale_w,
                          act_from=None if act_col is None else act_col // tn, side=side_key),
        grid=(ni, nj),
        in_specs=in_specs,
        out_specs=out_specs,
        out_shape=out_shape,
        input_output_aliases=aliases,
        scratch_shapes=list(scratch),
        compiler_params=_params(("arbitrary", "arbitrary")),
        name=name + {"first": "_first", "rest": "_rest", "all": ""}[mode],
    )(*args)


def _gelu_tanh(x):
    c = math.sqrt(2.0 / math.pi)
    inner = x * (c + (c * 0.044715) * (x * x))
    return (0.5 * x) * (1.0 + jnp.tanh(inner))


def _scale_body(acc, ins, outs, scratch, act=False):
    (inv_ref,), (o_ref,) = ins, outs
    t = acc * inv_ref[...]
    o_ref[...] = (_gelu_tanh(t) if act else t).astype(o_ref.dtype)


def _relu2_body(acc, ins, outs, scratch):
    (inv_ref,), (o_ref,) = ins, outs
    r = jnp.maximum(acc * inv_ref[...], 0.0)
    o_ref[...] = (r * r).astype(o_ref.dtype)


def _residual_norm_body(acc, ins, outs, scratch, *, d_model):
    (x_ref, g_ref), (x1_ref, x1g_ref, inv_ref), (ss_ref,) = ins, outs, scratch
    j = pl.program_id(1)
    x1 = x_ref[...] + acc
    x1_ref[...] = x1
    x1g_ref[...] = (x1 * g_ref[...]).astype(BF16)
    part = jnp.sum(x1 * x1, axis=-1, keepdims=True)

    @pl.when(j == 0)
    def _():
        ss_ref[...] = part

    @pl.when(j > 0)
    def _():
        ss_ref[...] += part

    @pl.when(j == pl.num_programs(1) - 1)
    def _():
        inv_ref[...] = lax.rsqrt(ss_ref[...] * (1.0 / d_model) + EPS)


PREP_ROWS = 256


def _inproj_rest_kernel(x_ref, g_ref, w_ref, z_prev_ref, o_ref, xg_scr, inv_scr,
                        *, n_row_blocks, prep_steps, act_from):
    del z_prev_ref
    r = pl.program_id(0)
    j = pl.program_id(1)
    slot_in, slot_out = lax.rem(r, 2), lax.rem(r + 1, 2)

    def project(act):
        acc = jnp.dot(xg_scr[slot_in], w_ref[...], preferred_element_type=F32)
        t = acc * inv_scr[slot_in]
        o_ref[...] = (_gelu_tanh(t) if act else t).astype(o_ref.dtype)

    def prep():
        x = x_ref[...]
        rows = pl.ds(pl.multiple_of(j * PREP_ROWS, PREP_ROWS), PREP_ROWS)
        ss = jnp.sum(x * x, axis=-1, keepdims=True)
        inv_scr[slot_out, rows, :] = lax.rsqrt(ss * (1.0 / x.shape[-1]) + EPS)
        xg_scr[slot_out, rows, :] = (x * g_ref[...]).astype(BF16)

    has_next = r < n_row_blocks - 1
    prepping = has_next & (j < prep_steps)

    @pl.when((r == 0) & prepping)
    def _():
        prep()

    for act in (False, True):
        variant = (r > 0) & ((j >= act_from) == act)

        @pl.when(variant & prepping)
        def _():
            project(act)
            prep()

        @pl.when(variant & jnp.logical_not(prepping))
        def _():
            project(act)


def _inproj(x2d, gain, w_in, act_col, tm=1024, tn=1024):
    m, k = x2d.shape
    n = w_in.shape[1]
    ni, nj = m // tm, n // tn
    prep_steps = tm // PREP_ROWS
    assert prep_steps <= nj and act_col % tn == 0
    xg0, inv0 = _prep(x2d, gain, tm)
    z_part, wb = _project(_scale_body, "in_proj", xg0, w_in, [(inv0, "row")],
                          [(jax.ShapeDtypeStruct((m, n), BF16), "tile")], tm=tm, tn=FIRST_TN,
                          mode="first", act_col=act_col)

    def x_map(r, j):
        return (jnp.minimum(r + 1, ni - 1) * prep_steps + jnp.minimum(j, prep_steps - 1), 0)

    def col(r, j):
        return jnp.where(r == 0, 0, j)

    return pl.pallas_call(
        functools.partial(_inproj_rest_kernel, n_row_blocks=ni, prep_steps=prep_steps,
                          act_from=act_col // tn),
        grid=(ni, nj),
        in_specs=[pl.BlockSpec((PREP_ROWS, k), x_map),
                  pl.BlockSpec((1, k), lambda r, j: (0, 0)),
                  pl.BlockSpec((k, tn), lambda r, j: (0, col(r, j))),
                  pl.BlockSpec(memory_space=pl.ANY)],
        out_specs=pl.BlockSpec((tm, tn), lambda r, j: (jnp.maximum(r, 1), col(r, j))),
        out_shape=jax.ShapeDtypeStruct((m, n), BF16),
        input_output_aliases={3: 0},
        scratch_shapes=[pltpu.VMEM((2, tm, k), BF16), pltpu.VMEM((2, tm, 1), F32)],
        compiler_params=_params(("arbitrary", "arbitrary")),
        name="in_proj_rest",
    )(x2d, gain.reshape(1, k), wb, z_part)


def _up(x1g, w1, inv, w_next, tm=1024, tn=1024):
    m, n = x1g.shape[0], w1.shape[1]
    ins, outs = [(inv, "row")], [(jax.ShapeDtypeStruct((m, n), BF16), "tile")]
    hid_part, w1b = _project(_relu2_body, "mlp_up", x1g, w1, ins, outs, tm=tm, tn=FIRST_TN,
                             mode="first")
    return _project(_relu2_body, "mlp_up", x1g, w1b, ins, outs, tm=tm, tn=tn, mode="rest",
                    prev=[hid_part], side=(w_next, None))


def _outproj(mix, w_outb, x2d, gain, tm=1024, tn=512):
    m, n = x2d.shape
    outs = [(jax.ShapeDtypeStruct((m, n), F32), "tile"),
            (jax.ShapeDtypeStruct((m, n), BF16), "tile"),
            (jax.ShapeDtypeStruct((m, 1), F32), "row")]
    return _project(functools.partial(_residual_norm_body, d_model=n), "out_proj", mix, w_outb,
                    [(x2d, "tile"), (gain.reshape(1, n), "col")], outs, tm=tm, tn=tn, mode="all",
                    scratch=[pltpu.VMEM((tm, 1), F32)])


def _down_kernel(a_ref, w_ref, x1_ref, o_ref):
    @pl.when(pl.program_id(2) == 0)
    def _():
        o_ref[...] = x1_ref[...]

    o_ref[...] += jnp.dot(a_ref[...], w_ref[...], preferred_element_type=F32)


def _down(hid, w2b, x1, tm=1024, tn=1024, tk=4096):
    m, k = hid.shape
    n = w2b.shape[1]
    return pl.pallas_call(
        _down_kernel,
        grid=(m // tm, n // tn, k // tk),
        in_specs=[pl.BlockSpec((tm, tk), lambda i, j, kk: (i, kk)),
                  pl.BlockSpec((tk, tn), lambda i, j, kk: (kk, j)),
                  pl.BlockSpec((tm, tn), lambda i, j, kk: (i, j))],
        out_specs=pl.BlockSpec((tm, tn), lambda i, j, kk: (i, j)),
        out_shape=jax.ShapeDtypeStruct((m, n), F32),
        compiler_params=_params(("arbitrary", "arbitrary", "arbitrary")),
        name="mlp_down",
    )(hid, w2b, x1)


def _t5_bucket(rel):
    nb = N_BUCKETS // 2
    max_exact = nb // 2
    ret = (rel > 0).astype(np.int32) * nb
    n = np.abs(rel)
    large = max_exact + (np.log(np.maximum(n, 1).astype(np.float32) / max_exact)
                         / math.log(MAX_DISTANCE / max_exact) * (nb - max_exact)).astype(np.int32)
    large = np.minimum(large, nb - 1)
    return ret + np.where(n < max_exact, n, large)


def _bucket_table():
    a = np.arange(BLOCK)[:, None]
    s = np.arange(3 * BLOCK)[None, :]
    rel = s - BLOCK - a
    return np.where(np.abs(rel) <= WINDOW, _t5_bucket(rel), -1).astype(np.int32)


def _bias_kernel(rb_ref, bk_ref, o_ref):
    h = pl.program_id(0)
    bk = bk_ref[...]
    acc = jnp.full(bk.shape, NEG, F32)
    for b in range(N_BUCKETS):
        acc = jnp.where(bk == b, rb_ref[b, h] * LOG2E, acc)
    col = lax.broadcasted_iota(jnp.int32, bk.shape, 1)
    o_ref[0] = jnp.where(col >= BLOCK, acc, NEG)
    o_ref[1] = acc
    o_ref[2] = jnp.where(col < 2 * BLOCK, acc, NEG)


def _bias_tables(rel_bias):
    return pl.pallas_call(
        _bias_kernel,
        grid=(N_Q_HEADS,),
        in_specs=[pl.BlockSpec(memory_space=pltpu.SMEM),
                  pl.BlockSpec((BLOCK, 3 * BLOCK), lambda h: (0, 0))],
        out_specs=pl.BlockSpec((3, None, BLOCK, 3 * BLOCK), lambda h: (0, h, 0, 0)),
        out_shape=jax.ShapeDtypeStruct((3, N_Q_HEADS, BLOCK, 3 * BLOCK), F32),
        compiler_params=_params(("arbitrary",)),
        name="bias_tables",
    )(rel_bias, jnp.asarray(_bucket_table()))


def _rms_scale(t):
    return lax.rsqrt(jnp.sum(t * t, axis=-1, keepdims=True) * (1.0 / t.shape[-1]) + EPS)


MIX_BLOCKS = 2
Z_COL_BLOCK = 1024


def _mix_kernel(sink_ref, q_ref, kp_ref, ko_ref, kn_ref, vp_ref, vo_ref, vn_ref,
                u0_ref, u1_ref, g0_ref, g1_ref, bias_ref, qg_ref, kg_ref,
                vg_ref, ws_ref, bt_ref, side_src, side_scale, mix_ref, side_dst, a_scr, g_scr,
                *, n_chunks):
    n = pl.program_id(1)
    nlast = pl.num_programs(1) - 1
    kfold = kg_ref[...] * qg_ref[...] * (HEAD_DIM ** -0.5 * LOG2E)

    kband = jnp.concatenate([kp_ref[...], ko_ref[...], kn_ref[...]], axis=0).astype(F32)
    vband = jnp.concatenate([vp_ref[...], vo_ref[...], vn_ref[...]], axis=0)
    ones = jnp.ones((3 * BLOCK, HEAD_DIM), BF16)
    for kh in range(N_KV_HEADS):
        k = kband[:, kh * HEAD_DIM:(kh + 1) * HEAD_DIM]
        kn_all = (k * _rms_scale(k) * kfold).astype(BF16)
        for sb in range(MIX_BLOCKS):
            tok = slice(sb * BLOCK, (sb + 1) * BLOCK)
            band = slice(sb * BLOCK, (sb + 3) * BLOCK)
            if sb == 0:
                variant = jnp.where(n == 0, 0, 1)
            elif sb == MIX_BLOCKS - 1:
                variant = jnp.where(n == nlast, 2, 1)
            else:
                variant = 1
            kn = kn_all[band]
            v1 = jnp.concatenate([vband[band, kh * HEAD_DIM:(kh + 1) * HEAD_DIM], ones], axis=1)
            qs = []
            for g in range(GQA_GROUP):
                h = kh * GQA_GROUP + g
                qh = q_ref[tok, h * HEAD_DIM:(h + 1) * HEAD_DIM].astype(F32)
                qs.append((qh * _rms_scale(qh)).astype(BF16))
            qstack = jnp.concatenate(qs, axis=0)
            s = lax.dot_general(qstack, kn, (((1,), (1,)), ((), ())),
                                preferred_element_type=F32)
            ps, sink_terms = [], []
            for g in range(GQA_GROUP):
                h = kh * GQA_GROUP + g
                sink = sink_ref[h] * LOG2E
                sg = s[g * BLOCK:(g + 1) * BLOCK] + bias_ref[variant, h]
                m = jnp.maximum(jnp.max(sg, axis=-1, keepdims=True), sink)
                ps.append(jnp.exp2(sg - m).astype(BF16))
                sink_terms.append(jnp.exp2(sink - m))
            pstack = jnp.concatenate(ps, axis=0)
            ol = jnp.dot(pstack, v1, preferred_element_type=F32)
            for g in range(GQA_GROUP):
                h = kh * GQA_GROUP + g
                rows = slice(g * BLOCK, (g + 1) * BLOCK)
                denom = ol[rows, HEAD_DIM:] + sink_terms[g]
                a_scr[tok, h * HEAD_DIM:(h + 1) * HEAD_DIM] = ol[rows, :HEAD_DIM] * (1.0 / denom)

    a = a_scr[...]
    mix_ref[:, :ATTN_W] = (a * _rms_scale(a)).astype(mix_ref.dtype)

    gu = jnp.concatenate([u0_ref[...], u1_ref[...]], axis=1).astype(F32)
    gv = jnp.concatenate([g0_ref[...], g1_ref[...]], axis=1).astype(F32)
    vn_ = (gv * _rms_scale(gv) * vg_ref[...]).astype(BF16)
    for sb in range(MIX_BLOCKS):
        tok = slice(sb * BLOCK, (sb + 1) * BLOCK)
        for h in range(GMLP_HEADS):
            sl = slice(h * HEAD_DIM, (h + 1) * HEAD_DIM)
            sv = jnp.dot(ws_ref[h], vn_[tok, sl], preferred_element_type=F32) + bt_ref[:, h:h + 1]
            g_scr[tok, sl] = gu[tok, sl] * sv
    gg = g_scr[...]
    mix_ref[:, ATTN_W:] = (gg * _rms_scale(gg)).astype(mix_ref.dtype)
    _cast_chunk(pl.program_id(0) * pl.num_programs(1) + n, n_chunks, side_src, side_scale, side_dst)


def _mix(z3, bias, sink, q_gain, k_gain, v_gain, w_s, b_t, w_next, scale_next):
    bsz, seq, _ = z3.shape
    nblk = seq // BLOCK
    rows = MIX_BLOCKS * BLOCK
    assert nblk >= 2 and nblk % MIX_BLOCKS == 0
    kcol = ATTN_W // KV_W
    vcol = kcol + 1
    ucol = (ATTN_W + 2 * KV_W) // Z_COL_BLOCK
    gcol = ucol + GMLP_W // Z_COL_BLOCK
    assert (ATTN_W + 2 * KV_W) % Z_COL_BLOCK == 0 and GMLP_W == 2 * Z_COL_BLOCK
    n_steps = nblk // MIX_BLOCKS
    n_chunks, s_in, s_args, s_out_spec, s_out_shape = _side_specs(
        w_next, scale_next, bsz * n_steps, lambda b, n: b * n_steps + n)

    def own(width, col):
        return pl.BlockSpec((None, rows, width), lambda b, n: (b, n, col))

    def edge(width, col, after):
        def imap(b, n):
            blk = n * MIX_BLOCKS + (MIX_BLOCKS if after else -1)
            return (b, jnp.clip(blk, 0, nblk - 1), col)
        return pl.BlockSpec((None, BLOCK, width), imap)

    def full(shape):
        return pl.BlockSpec(shape, lambda b, n: (0,) * len(shape))

    in_specs = [
        pl.BlockSpec(memory_space=pltpu.SMEM),
        own(ATTN_W, 0),
        edge(KV_W, kcol, False), own(KV_W, kcol), edge(KV_W, kcol, True),
        edge(KV_W, vcol, False), own(KV_W, vcol), edge(KV_W, vcol, True),
        own(Z_COL_BLOCK, ucol), own(Z_COL_BLOCK, ucol + 1),
        own(Z_COL_BLOCK, gcol), own(Z_COL_BLOCK, gcol + 1),
        pl.BlockSpec(bias.shape, lambda b, n: (0, 0, 0, 0), pipeline_mode=pl.Buffered(1)),
        full((1, HEAD_DIM)), full((1, HEAD_DIM)),
        full((1, GMLP_W)), full(w_s.shape), full(b_t.shape),
    ] + s_in
    return pl.pallas_call(
        functools.partial(_mix_kernel, n_chunks=n_chunks),
        grid=(bsz, n_steps),
        in_specs=in_specs,
        out_specs=[pl.BlockSpec((None, rows, ATTN_W + GMLP_W), lambda b, n: (b, n, 0)), s_out_spec],
        out_shape=[jax.ShapeDtypeStruct((bsz, seq, ATTN_W + GMLP_W), BF16), s_out_shape],
        scratch_shapes=[pltpu.VMEM((rows, ATTN_W), F32), pltpu.VMEM((rows, GMLP_W), F32)],
        compiler_params=_params(("arbitrary", "arbitrary")),
        name="mix",
    )(sink, z3, z3, z3, z3, z3, z3, z3, z3, z3, z3, z3, bias,
      q_gain.reshape(1, HEAD_DIM), k_gain.reshape(1, HEAD_DIM),
      v_gain.reshape(1, GMLP_W), w_s, b_t, *s_args)


def kernel(x, norm1, w_in, q_gain, k_gain, rel_bias, attn_sink, attn_out_gain,
           gmlp_v_gain, gmlp_w_s, gmlp_b_s, gmlp_out_gain, w_out, norm2, w1, w2):
    bsz, seq, d = x.shape
    depth = norm1.shape[0]
    bias = _bias_tables(rel_bias)
    x2d = x.reshape(bsz * seq, d)
    for l in range(depth):
        mix_gain = jnp.concatenate([attn_out_gain[l], gmlp_out_gain[l]])
        z = _inproj(x2d, norm1[l], w_in[l], ATTN_W + 2 * KV_W)
        mix, w_outb = _mix(z.reshape(bsz, seq, -1), bias, attn_sink[l], q_gain[l], k_gain[l],
                           gmlp_v_gain[l], gmlp_w_s[l].astype(BF16), gmlp_b_s[l].T, w_out[l], mix_gain)
        x1, x1g, inv2 = _outproj(mix.reshape(bsz * seq, -1), w_outb, x2d, norm2[l])
        hid, w2b = _up(x1g, w1[l], inv2, w2[l])
        x2d = _down(hid, w2b, x1)
    return x2d.reshape(bsz, seq, d)
```

```python
import functools
import math

import jax
import jax.numpy as jnp
import numpy as np
from jax import lax
from jax.experimental import pallas as pl
from jax.experimental.pallas import tpu as pltpu

HEAD_DIM = 128
N_Q_HEADS = 16
N_KV_HEADS = 4
GQA_GROUP = N_Q_HEADS // N_KV_HEADS
ATTN_W = N_Q_HEADS * HEAD_DIM
KV_W = N_KV_HEADS * HEAD_DIM
WINDOW = 128
BLOCK = 128
N_BUCKETS = 32
MAX_DISTANCE = 128
GMLP_HEADS = 16
GMLP_W = GMLP_HEADS * HEAD_DIM
EPS = 1e-6
NEG = -1e30
LOG2E = math.log2(math.e)

F32 = jnp.float32
BF16 = jnp.bfloat16

V7X_VMEM_LIMIT_BYTES = 60 * 1024 * 1024


def _params(sem):
    return pltpu.CompilerParams(dimension_semantics=sem,
                                vmem_limit_bytes=V7X_VMEM_LIMIT_BYTES)


def _prep_kernel(x_ref, g_ref, xg_ref, inv_ref):
    x = x_ref[...]
    d = x.shape[-1]
    ss = jnp.sum(x * x, axis=-1, keepdims=True)
    inv_ref[...] = lax.rsqrt(ss * (1.0 / d) + EPS)
    xg_ref[...] = (x * g_ref[...]).astype(BF16)


def _prep(x2d, gain, n_rows, rows=256):
    m, d = n_rows, x2d.shape[1]
    return pl.pallas_call(
        _prep_kernel,
        grid=(m // rows,),
        in_specs=[pl.BlockSpec((rows, d), lambda i: (i, 0)),
                  pl.BlockSpec((1, d), lambda i: (0, 0))],
        out_specs=[pl.BlockSpec((rows, d), lambda i: (i, 0)),
                   pl.BlockSpec((rows, 1), lambda i: (i, 0))],
        out_shape=[jax.ShapeDtypeStruct((m, d), BF16),
                   jax.ShapeDtypeStruct((m, 1), F32)],
        compiler_params=_params(("arbitrary",)),
        name="prep_norm",
    )(x2d, gain.reshape(1, d))


FIRST_TN = 512


def _cast_chunk(step, n_chunks, src_ref, scale_ref, dst_ref):
    @pl.when(step < n_chunks)
    def _():
        w = src_ref[...]
        if scale_ref is not None:
            w = w * scale_ref[...]
        dst_ref[...] = w.astype(BF16)


def _proj_kernel(a_ref, w_ref, *refs, body, n_in, n_prev, n_out, cast_w, act_from, side):
    ins = refs[:n_in]
    refs = refs[n_in:]
    if side is not None:
        side_src, refs = refs[0], refs[1:]
        side_scale = None
        if side[1]:
            side_scale, refs = refs[0], refs[1:]
    outs = refs[n_prev:n_prev + n_out]
    rest = refs[n_prev + n_out:]
    if cast_w:
        wb_ref, rest = rest[0], rest[1:]
    if side is not None:
        side_dst, rest = rest[0], rest[1:]

    def run(**body_kw):
        w = w_ref[...]
        if cast_w:
            w = w.astype(BF16)
            wb_ref[...] = w
        acc = jnp.dot(a_ref[...], w, preferred_element_type=F32)
        body(acc, ins, outs, rest, **body_kw)

    if act_from is None:
        run()
    else:
        j = pl.program_id(1)
        pl.when(j < act_from)(functools.partial(run, act=False))
        pl.when(j >= act_from)(functools.partial(run, act=True))
    if side is not None:
        step = pl.program_id(0) * pl.num_programs(1) + pl.program_id(1)
        _cast_chunk(step, side[0], side_src, side_scale, side_dst)


SIDE_CHUNKS = 128


def _side_specs(w_next, scale_next, n_steps, step_of):
    k2, n2 = w_next.shape
    n_chunks = min(SIDE_CHUNKS, 1 << (n_steps.bit_length() - 1))
    rows = k2 // n_chunks
    assert rows * n_chunks == k2 and rows % 16 == 0

    def imap(*ids):
        return (jnp.minimum(step_of(*ids), n_chunks - 1), 0)

    in_specs, args = [pl.BlockSpec((rows, n2), imap)], [w_next]
    if scale_next is not None:
        in_specs.append(pl.BlockSpec((rows, 1), imap))
        args.append(scale_next.reshape(k2, 1))
    return n_chunks, in_specs, args, pl.BlockSpec((rows, n2), imap), jax.ShapeDtypeStruct((k2, n2), BF16)


def _project(body, name, a, w, ins, outs, *, tm, tn, mode, prev=None, scratch=(), act_col=None,
             side=None):
    m, k = a.shape
    n = w.shape[1]
    first = mode == "first"
    ioff = 1 if mode == "rest" else 0
    ni = {"first": 1, "rest": m // tm - 1, "all": m // tm}[mode]
    nj = n // tn

    def spec(kind):
        if kind == "row":
            return pl.BlockSpec((tm, 1), lambda i, j: (i + ioff, 0))
        if kind == "col":
            return pl.BlockSpec((1, tn), lambda i, j: (0, j))
        return pl.BlockSpec((tm, tn), lambda i, j: (i + ioff, j))

    a_mode = dict(pipeline_mode=pl.Buffered(1)) if first else {}
    in_specs = [pl.BlockSpec((tm, k), lambda i, j: (i + ioff, 0), **a_mode),
                pl.BlockSpec((k, tn), lambda i, j: (0, j))]
    args = [a, w]
    in_specs += [spec(kind) for _, kind in ins]
    args += [arr for arr, _ in ins]
    side_key = None
    if side is not None:
        n_chunks, s_in, s_args, s_out_spec, s_out_shape = _side_specs(
            side[0], side[1], ni * nj, lambda i, j: i * nj + j)
        in_specs += s_in
        args += s_args
        side_key = (n_chunks, side[1] is not None)
    out_specs = [spec(kind) for _, kind in outs]
    out_shape = [sds for sds, _ in outs]
    aliases = {}
    if first:
        out_specs.append(pl.BlockSpec((k, tn), lambda i, j: (0, j)))
        out_shape.append(jax.ShapeDtypeStruct((k, n), BF16))
    elif mode == "rest":
        for idx, p in enumerate(prev):
            aliases[len(args)] = idx
            in_specs.append(pl.BlockSpec(memory_space=pl.ANY))
            args.append(p)
    if side is not None:
        out_specs.append(s_out_spec)
        out_shape.append(s_out_shape)
    assert act_col is None or act_col % tn == 0
    return pl.pallas_call(
        functools.partial(_proj_kernel, body=body, n_in=len(ins), n_prev=len(aliases),
                          n_out=len(outs), cast_w=first,
                          act_from=None if act_col is None else act_col // tn, side=side_key),
        grid=(ni, nj),
        in_specs=in_specs,
        out_specs=out_specs,
        out_shape=out_shape,
        input_output_aliases=aliases,
        scratch_shapes=list(scratch),
        compiler_params=_params(("arbitrary", "arbitrary")),
        name=name + {"first": "_first", "rest": "_rest", "all": ""}[mode],
    )(*args)


def _gelu_tanh(x):
    c = math.sqrt(2.0 / math.pi)
    inner = x * (c + (c * 0.044715) * (x * x))
    return (0.5 * x) * (1.0 + jnp.tanh(inner))


def _scale_body(acc, ins, outs, scratch, act=False):
    (inv_ref,), (o_ref,) = ins, outs
    t = acc * inv_ref[...]
    o_ref[...] = (_gelu_tanh(t) if act else t).astype(o_ref.dtype)


def _relu2_body(acc, ins, outs, scratch):
    (inv_ref,), (o_ref,) = ins, outs
    r = jnp.maximum(acc * inv_ref[...], 0.0)
    o_ref[...] = (r * r).astype(o_ref.dtype)


def _residual_norm_body(acc, ins, outs, scratch, *, d_model):
    (x_ref, g_ref), (x1_ref, x1g_ref, inv_ref), (ss_ref,) = ins, outs, scratch
    j = pl.program_id(1)
    x1 = x_ref[...] + acc
    x1_ref[...] = x1
    x1g_ref[...] = (x1 * g_ref[...]).astype(BF16)
    part = jnp.sum(x1 * x1, axis=-1, keepdims=True)

    @pl.when(j == 0)
    def _():
        ss_ref[...] = part

    @pl.when(j > 0)
    def _():
        ss_ref[...] += part

    @pl.when(j == pl.num_programs(1) - 1)
    def _():
        inv_ref[...] = lax.rsqrt(ss_ref[...] * (1.0 / d_model) + EPS)


PREP_ROWS = 256


def _inproj_rest_kernel(x_ref, g_ref, w_ref, z_prev_ref, o_ref, xg_scr, inv_scr,
                        *, n_row_blocks, prep_steps, act_from):
    del z_prev_ref
    r = pl.program_id(0)
    j = pl.program_id(1)
    slot_in, slot_out = lax.rem(r, 2), lax.rem(r + 1, 2)

    def project(act):
        acc = jnp.dot(xg_scr[slot_in], w_ref[...], preferred_element_type=F32)
        t = acc * inv_scr[slot_in]
        o_ref[...] = (_gelu_tanh(t) if act else t).astype(o_ref.dtype)

    def prep():
        x = x_ref[...]
        rows = pl.ds(pl.multiple_of(j * PREP_ROWS, PREP_ROWS), PREP_ROWS)
        ss = jnp.sum(x * x, axis=-1, keepdims=True)
        inv_scr[slot_out, rows, :] = lax.rsqrt(ss * (1.0 / x.shape[-1]) + EPS)
        xg_scr[slot_out, rows, :] = (x * g_ref[...]).astype(BF16)

    has_next = r < n_row_blocks - 1
    prepping = has_next & (j < prep_steps)

    @pl.when((r == 0) & prepping)
    def _():
        prep()

    for act in (False, True):
        variant = (r > 0) & ((j >= act_from) == act)

        @pl.when(variant & prepping)
        def _():
            project(act)
            prep()

        @pl.when(variant & jnp.logical_not(prepping))
        def _():
            project(act)


def _inproj(x2d, gain, w_in, act_col, tm=1024, tn=1024):
    m, k = x2d.shape
    n = w_in.shape[1]
    ni, nj = m // tm, n // tn
    prep_steps = tm // PREP_ROWS
    assert prep_steps <= nj and act_col % tn == 0
    xg0, inv0 = _prep(x2d, gain, tm)
    z_part, wb = _project(_scale_body, "in_proj", xg0, w_in, [(inv0, "row")],
                          [(jax.ShapeDtypeStruct((m, n), BF16), "tile")], tm=tm, tn=FIRST_TN,
                          mode="first", act_col=act_col)

    def x_map(r, j):
        return (jnp.minimum(r + 1, ni - 1) * prep_steps + jnp.minimum(j, prep_steps - 1), 0)

    def col(r, j):
        return jnp.where(r == 0, 0, j)

    return pl.pallas_call(
        functools.partial(_inproj_rest_kernel, n_row_blocks=ni, prep_steps=prep_steps,
                          act_from=act_col // tn),
        grid=(ni, nj),
        in_specs=[pl.BlockSpec((PREP_ROWS, k), x_map),
                  pl.BlockSpec((1, k), lambda r, j: (0, 0)),
                  pl.BlockSpec((k, tn), lambda r, j: (0, col(r, j))),
                  pl.BlockSpec(memory_space=pl.ANY)],
        out_specs=pl.BlockSpec((tm, tn), lambda r, j: (jnp.maximum(r, 1), col(r, j))),
        out_shape=jax.ShapeDtypeStruct((m, n), BF16),
        input_output_aliases={3: 0},
        scratch_shapes=[pltpu.VMEM((2, tm, k), BF16), pltpu.VMEM((2, tm, 1), F32)],
        compiler_params=_params(("arbitrary", "arbitrary")),
        name="in_proj_rest",
    )(x2d, gain.reshape(1, k), wb, z_part)


def _up(x1g, w1, inv, w_next, tm=1024, tn=1024):
    m, n = x1g.shape[0], w1.shape[1]
    ins, outs = [(inv, "row")], [(jax.ShapeDtypeStruct((m, n), BF16), "tile")]
    hid_part, w1b = _project(_relu2_body, "mlp_up", x1g, w1, ins, outs, tm=tm, tn=FIRST_TN,
                             mode="first")
    return _project(_relu2_body, "mlp_up", x1g, w1b, ins, outs, tm=tm, tn=tn, mode="rest",
                    prev=[hid_part], side=(w_next, None))


def _outproj(mix, w_outb, x2d, gain, tm=1024, tn=512):
    m, n = x2d.shape
    outs = [(jax.ShapeDtypeStruct((m, n), F32), "tile"),
            (jax.ShapeDtypeStruct((m, n), BF16), "tile"),
            (jax.ShapeDtypeStruct((m, 1), F32), "row")]
    return _project(functools.partial(_residual_norm_body, d_model=n), "out_proj", mix, w_outb,
                    [(x2d, "tile"), (gain.reshape(1, n), "col")], outs, tm=tm, tn=tn, mode="all",
                    scratch=[pltpu.VMEM((tm, 1), F32)])


def _down_kernel(a_ref, w_ref, x1_ref, o_ref):
    @pl.when(pl.program_id(2) == 0)
    def _():
        o_ref[...] = x1_ref[...]

    o_ref[...] += jnp.dot(a_ref[...], w_ref[...], preferred_element_type=F32)


def _down(hid, w2b, x1, tm=1024, tn=1024, tk=4096):
    m, k = hid.shape
    n = w2b.shape[1]
    return pl.pallas_call(
        _down_kernel,
        grid=(m // tm, n // tn, k // tk),
        in_specs=[pl.BlockSpec((tm, tk), lambda i, j, kk: (i, kk)),
                  pl.BlockSpec((tk, tn), lambda i, j, kk: (kk, j)),
                  pl.BlockSpec((tm, tn), lambda i, j, kk: (i, j))],
        out_specs=pl.BlockSpec((tm, tn), lambda i, j, kk: (i, j)),
        out_shape=jax.ShapeDtypeStruct((m, n), F32),
        compiler_params=_params(("arbitrary", "arbitrary", "arbitrary")),
        name="mlp_down",
    )(hid, w2b, x1)


def _t5_bucket(rel):
    nb = N_BUCKETS // 2
    max_exact = nb // 2
    ret = (rel > 0).astype(np.int32) * nb
    n = np.abs(rel)
    large = max_exact + (np.log(np.maximum(n, 1).astype(np.float32) / max_exact)
                         / math.log(MAX_DISTANCE / max_exact) * (nb - max_exact)).astype(np.int32)
    large = np.minimum(large, nb - 1)
    return ret + np.where(n < max_exact, n, large)


def _bucket_table():
    a = np.arange(BLOCK)[:, None]
    s = np.arange(3 * BLOCK)[None, :]
    rel = s - BLOCK - a
    return np.where(np.abs(rel) <= WINDOW, _t5_bucket(rel), -1).astype(np.int32)


def _bias_kernel(rb_ref, bk_ref, o_ref):
    h = pl.program_id(0)
    bk = bk_ref[...]
    acc = jnp.full(bk.shape, NEG, F32)
    for b in range(N_BUCKETS):
        acc = jnp.where(bk == b, rb_ref[b, h] * LOG2E, acc)
    col = lax.broadcasted_iota(jnp.int32, bk.shape, 1)
    o_ref[0] = jnp.where(col >= BLOCK, acc, NEG)
    o_ref[1] = acc
    o_ref[2] = jnp.where(col < 2 * BLOCK, acc, NEG)


def _bias_tables(rel_bias):
    return pl.pallas_call(
        _bias_kernel,
        grid=(N_Q_HEADS,),
        in_specs=[pl.BlockSpec(memory_space=pltpu.SMEM),
                  pl.BlockSpec((BLOCK, 3 * BLOCK), lambda h: (0, 0))],
        out_specs=pl.BlockSpec((3, None, BLOCK, 3 * BLOCK), lambda h: (0, h, 0, 0)),
        out_shape=jax.ShapeDtypeStruct((3, N_Q_HEADS, BLOCK, 3 * BLOCK), F32),
        compiler_params=_params(("arbitrary",)),
        name="bias_tables",
    )(rel_bias, jnp.asarray(_bucket_table()))


def _rms_scale(t):
    return lax.rsqrt(jnp.sum(t * t, axis=-1, keepdims=True) * (1.0 / t.shape[-1]) + EPS)


MIX_BLOCKS = 2
Z_COL_BLOCK = 1024


def _mix_kernel(sink_ref, q_ref, kp_ref, ko_ref, kn_ref, vp_ref, vo_ref, vn_ref,
                u0_ref, u1_ref, g0_ref, g1_ref, bias_ref, qg_ref, kg_ref,
                vg_ref, ws_ref, bt_ref, side_src, side_scale, mix_ref, side_dst, a_scr, g_scr,
                *, n_chunks):
    n = pl.program_id(1)
    nlast = pl.num_programs(1) - 1
    kfold = kg_ref[...] * qg_ref[...] * (HEAD_DIM ** -0.5 * LOG2E)

    kband = jnp.concatenate([kp_ref[...], ko_ref[...], kn_ref[...]], axis=0).astype(F32)
    vband = jnp.concatenate([vp_ref[...], vo_ref[...], vn_ref[...]], axis=0)
    ones = jnp.ones((3 * BLOCK, HEAD_DIM), BF16)
    for kh in range(N_KV_HEADS):
        k = kband[:, kh * HEAD_DIM:(kh + 1) * HEAD_DIM]
        kn_all = (k * _rms_scale(k) * kfold).astype(BF16)
        for sb in range(MIX_BLOCKS):
            tok = slice(sb * BLOCK, (sb + 1) * BLOCK)
            band = slice(sb * BLOCK, (sb + 3) * BLOCK)
            if sb == 0:
                variant = jnp.where(n == 0, 0, 1)
            elif sb == MIX_BLOCKS - 1:
                variant = jnp.where(n == nlast, 2, 1)
            else:
                variant = 1
            kn = kn_all[band]
            v1 = jnp.concatenate([vband[band, kh * HEAD_DIM:(kh + 1) * HEAD_DIM], ones], axis=1)
            qs = []
            for g in range(GQA_GROUP):
                h = kh * GQA_GROUP + g
                qh = q_ref[tok, h * HEAD_DIM:(h + 1) * HEAD_DIM].astype(F32)
                qs.append((qh * _rms_scale(qh)).astype(BF16))
            qstack = jnp.concatenate(qs, axis=0)
            s = lax.dot_general(qstack, kn, (((1,), (1,)), ((), ())),
                                preferred_element_type=F32)
            ps, sink_terms = [], []
            for g in range(GQA_GROUP):
                h = kh * GQA_GROUP + g
                sink = sink_ref[h] * LOG2E
                sg = s[g * BLOCK:(g + 1) * BLOCK] + bias_ref[variant, h]
                m = jnp.maximum(jnp.max(sg, axis=-1, keepdims=True), sink)
                ps.append(jnp.exp2(sg - m).astype(BF16))
                sink_terms.append(jnp.exp2(sink - m))
            pstack = jnp.concatenate(ps, axis=0)
            ol = jnp.dot(pstack, v1, preferred_element_type=F32)
            for g in range(GQA_GROUP):
                h = kh * GQA_GROUP + g
                rows = slice(g * BLOCK, (g + 1) * BLOCK)
                denom = ol[rows, HEAD_DIM:] + sink_terms[g]
                a_scr[tok, h * HEAD_DIM:(h + 1) * HEAD_DIM] = ol[rows, :HEAD_DIM] * (1.0 / denom)

    a = a_scr[...]
    mix_ref[:, :ATTN_W] = (a * _rms_scale(a)).astype(mix_ref.dtype)

    gu = jnp.concatenate([u0_ref[...], u1_ref[...]], axis=1).astype(F32)
    gv = jnp.concatenate([g0_ref[...], g1_ref[...]], axis=1).astype(F32)
    vn_ = (gv * _rms_scale(gv) * vg_ref[...]).astype(BF16)
    for sb in range(MIX_BLOCKS):
        tok = slice(sb * BLOCK, (sb + 1) * BLOCK)
        for h in range(GMLP_HEADS):
            sl = slice(h * HEAD_DIM, (h + 1) * HEAD_DIM)
            sv = jnp.dot(ws_ref[h], vn_[tok, sl], preferred_element_type=F32) + bt_ref[:, h:h + 1]
            g_scr[tok, sl] = gu[tok, sl] * sv
    gg = g_scr[...]
    mix_ref[:, ATTN_W:] = (gg * _rms_scale(gg)).astype(mix_ref.dtype)
    _cast_chunk(pl.program_id(0) * pl.num_programs(1) + n, n_chunks, side_src, side_scale, side_dst)


def _mix(z3, bias, sink, q_gain, k_gain, v_gain, w_s, b_t, w_next, scale_next):
    bsz, seq, _ = z3.shape
    nblk = seq // BLOCK
    rows = MIX_BLOCKS * BLOCK
    assert nblk >= 2 and nblk % MIX_BLOCKS == 0
    kcol = ATTN_W // KV_W
    vcol = kcol + 1
    ucol = (ATTN_W + 2 * KV_W) // Z_COL_BLOCK
    gcol = ucol + GMLP_W // Z_COL_BLOCK
    assert (ATTN_W + 2 * KV_W) % Z_COL_BLOCK == 0 and GMLP_W == 2 * Z_COL_BLOCK
    n_steps = nblk // MIX_BLOCKS
    n_chunks, s_in, s_args, s_out_spec, s_out_shape = _side_specs(
        w_next, scale_next, bsz * n_steps, lambda b, n: b * n_steps + n)

    def own(width, col):
        return pl.BlockSpec((None, rows, width), lambda b, n: (b, n, col))

    def edge(width, col, after):
        def imap(b, n):
            blk = n * MIX_BLOCKS + (MIX_BLOCKS if after else -1)
            return (b, jnp.clip(blk, 0, nblk - 1), col)
        return pl.BlockSpec((None, BLOCK, width), imap)

    def full(shape):
        return pl.BlockSpec(shape, lambda b, n: (0,) * len(shape))

    in_specs = [
        pl.BlockSpec(memory_space=pltpu.SMEM),
        own(ATTN_W, 0),
        edge(KV_W, kcol, False), own(KV_W, kcol), edge(KV_W, kcol, True),
        edge(KV_W, vcol, False), own(KV_W, vcol), edge(KV_W, vcol, True),
        own(Z_COL_BLOCK, ucol), own(Z_COL_BLOCK, ucol + 1),
        own(Z_COL_BLOCK, gcol), own(Z_COL_BLOCK, gcol + 1),
        pl.BlockSpec(bias.shape, lambda b, n: (0, 0, 0, 0), pipeline_mode=pl.Buffered(1)),
        full((1, HEAD_DIM)), full((1, HEAD_DIM)),
        full((1, GMLP_W)), full(w_s.shape), full(b_t.shape),
    ] + s_in
    return pl.pallas_call(
        functools.partial(_mix_kernel, n_chunks=n_chunks),
        grid=(bsz, n_steps),
        in_specs=in_specs,
        out_specs=[pl.BlockSpec((None, rows, ATTN_W + GMLP_W), lambda b, n: (b, n, 0)), s_out_spec],
        out_shape=[jax.ShapeDtypeStruct((bsz, seq, ATTN_W + GMLP_W), BF16), s_out_shape],
        scratch_shapes=[pltpu.VMEM((rows, ATTN_W), F32), pltpu.VMEM((rows, GMLP_W), F32)],
        compiler_params=_params(("arbitrary", "arbitrary")),
        name="mix",
    )(sink, z3, z3, z3, z3, z3, z3, z3, z3, z3, z3, z3, bias,
      q_gain.reshape(1, HEAD_DIM), k_gain.reshape(1, HEAD_DIM),
      v_gain.reshape(1, GMLP_W), w_s, b_t, *s_args)


def kernel(x, norm1, w_in, q_gain, k_gain, rel_bias, attn_sink, attn_out_gain,
           gmlp_v_gain, gmlp_w_s, gmlp_b_s, gmlp_out_gain, w_out, norm2, w1, w2):
    bsz, seq, d = x.shape
    depth = norm1.shape[0]
    bias = _bias_tables(rel_bias)
    x2d = x.reshape(bsz * seq, d)
    for l in range(depth):
        mix_gain = jnp.concatenate([attn_out_gain[l], gmlp_out_gain[l]])
        z = _inproj(x2d, norm1[l], w_in[l], ATTN_W + 2 * KV_W)
        mix, w_outb = _mix(z.reshape(bsz, seq, -1), bias, attn_sink[l], q_gain[l], k_gain[l],
                           gmlp_v_gain[l], gmlp_w_s[l].astype(BF16), gmlp_b_s[l].T, w_out[l], mix_gain)
        x1, x1g, inv2 = _outproj(mix.reshape(bsz * seq, -1), w_outb, x2d, norm2[l])
        hid, w2b = _up(x1g, w1[l], inv2, w2[l])
        x2d = _down(hid, w2b, x1)
    return x2d.reshape(bsz, seq, d)
```

```python
import functools
import math

import jax
import jax.numpy as jnp
import numpy as np
from jax import lax
from jax.experimental import pallas as pl
from jax.experimental.pallas import tpu as pltpu

HEAD_DIM = 128
N_Q_HEADS = 16
N_KV_HEADS = 4
GQA_GROUP = N_Q_HEADS // N_KV_HEADS
ATTN_W = N_Q_HEADS * HEAD_DIM
KV_W = N_KV_HEADS * HEAD_DIM
WINDOW = 128
BLOCK = 128
N_BUCKETS = 32
MAX_DISTANCE = 128
GMLP_HEADS = 16
GMLP_W = GMLP_HEADS * HEAD_DIM
EPS = 1e-6
NEG = -1e30
LOG2E = math.log2(math.e)

F32 = jnp.float32
BF16 = jnp.bfloat16

V7X_VMEM_LIMIT_BYTES = 60 * 1024 * 1024


def _params(sem):
    return pltpu.CompilerParams(dimension_semantics=sem,
                                vmem_limit_bytes=V7X_VMEM_LIMIT_BYTES)


def _prep_kernel(x_ref, g_ref, xg_ref, inv_ref):
    x = x_ref[...]
    d = x.shape[-1]
    ss = jnp.sum(x * x, axis=-1, keepdims=True)
    inv_ref[...] = lax.rsqrt(ss * (1.0 / d) + EPS)
    xg_ref[...] = (x * g_ref[...]).astype(BF16)


def _prep(x2d, gain, n_rows, rows=256):
    m, d = n_rows, x2d.shape[1]
    return pl.pallas_call(
        _prep_kernel,
        grid=(m // rows,),
        in_specs=[pl.BlockSpec((rows, d), lambda i: (i, 0)),
                  pl.BlockSpec((1, d), lambda i: (0, 0))],
        out_specs=[pl.BlockSpec((rows, d), lambda i: (i, 0)),
                   pl.BlockSpec((rows, 1), lambda i: (i, 0))],
        out_shape=[jax.ShapeDtypeStruct((m, d), BF16),
                   jax.ShapeDtypeStruct((m, 1), F32)],
        compiler_params=_params(("arbitrary",)),
        name="prep_norm",
    )(x2d, gain.reshape(1, d))


FIRST_TN = 512


def _cast_chunk(step, n_chunks, src_ref, scale_ref, dst_ref):
    @pl.when(step < n_chunks)
    def _():
        w = src_ref[...]
        if scale_ref is not None:
            w = w * scale_ref[...]
        dst_ref[...] = w.astype(BF16)


def _proj_kernel(a_ref, w_ref, *refs, body, n_in, n_prev, n_out, cast_w, act_from, side):
    ins = refs[:n_in]
    refs = refs[n_in:]
    if side is not None:
        side_src, refs = refs[0], refs[1:]
        side_scale = None
        if side[1]:
            side_scale, refs = refs[0], refs[1:]
    outs = refs[n_prev:n_prev + n_out]
    rest = refs[n_prev + n_out:]
    if cast_w:
        wb_ref, rest = rest[0], rest[1:]
    if side is not None:
        side_dst, rest = rest[0], rest[1:]

    def run(**body_kw):
        w = w_ref[...]
        if cast_w:
            w = w.astype(BF16)
            wb_ref[...] = w
        acc = jnp.dot(a_ref[...], w, preferred_element_type=F32)
        body(acc, ins, outs, rest, **body_kw)

    if act_from is None:
        run()
    else:
        j = pl.program_id(1)
        pl.when(j < act_from)(functools.partial(run, act=False))
        pl.when(j >= act_from)(functools.partial(run, act=True))
    if side is not None:
        step = pl.program_id(0) * pl.num_programs(1) + pl.program_id(1)
        _cast_chunk(step, side[0], side_src, side_scale, side_dst)


SIDE_CHUNKS = 128


def _side_specs(w_next, scale_next, n_steps, step_of):
    k2, n2 = w_next.shape
    n_chunks = min(SIDE_CHUNKS, 1 << (n_steps.bit_length() - 1))
    rows = k2 // n_chunks
    assert rows * n_chunks == k2 and rows % 16 == 0

    def imap(*ids):
        return (jnp.minimum(step_of(*ids), n_chunks - 1), 0)

    in_specs, args = [pl.BlockSpec((rows, n2), imap)], [w_next]
    if scale_next is not None:
        in_specs.append(pl.BlockSpec((rows, 1), imap))
        args.append(scale_next.reshape(k2, 1))
    return n_chunks, in_specs, args, pl.BlockSpec((rows, n2), imap), jax.ShapeDtypeStruct((k2, n2), BF16)


def _project(body, name, a, w, ins, outs, *, tm, tn, mode, prev=None, scratch=(), act_col=None,
             side=None):
    m, k = a.shape
    n = w.shape[1]
    first = mode == "first"
    ioff = 1 if mode == "rest" else 0
    ni = {"first": 1, "rest": m // tm - 1, "all": m // tm}[mode]
    nj = n // tn

    def spec(kind):
        if kind == "row":
            return pl.BlockSpec((tm, 1), lambda i, j: (i + ioff, 0))
        if kind == "col":
            return pl.BlockSpec((1, tn), lambda i, j: (0, j))
        return pl.BlockSpec((tm, tn), lambda i, j: (i + ioff, j))

    a_mode = dict(pipeline_mode=pl.Buffered(1)) if first else {}
    in_specs = [pl.BlockSpec((tm, k), lambda i, j: (i + ioff, 0), **a_mode),
                pl.BlockSpec((k, tn), lambda i, j: (0, j))]
    args = [a, w]
    in_specs += [spec(kind) for _, kind in ins]
    args += [arr for arr, _ in ins]
    side_key = None
    if side is not None:
        n_chunks, s_in, s_args, s_out_spec, s_out_shape = _side_specs(
            side[0], side[1], ni * nj, lambda i, j: i * nj + j)
        in_specs += s_in
        args += s_args
        side_key = (n_chunks, side[1] is not None)
    out_specs = [spec(kind) for _, kind in outs]
    out_shape = [sds for sds, _ in outs]
    aliases = {}
    if first:
        out_specs.append(pl.BlockSpec((k, tn), lambda i, j: (0, j)))
        out_shape.append(jax.ShapeDtypeStruct((k, n), BF16))
    elif mode == "rest":
        for idx, p in enumerate(prev):
            aliases[len(args)] = idx
            in_specs.append(pl.BlockSpec(memory_space=pl.ANY))
            args.append(p)
    if side is not None:
        out_specs.append(s_out_spec)
        out_shape.append(s_out_shape)
    assert act_col is None or act_col % tn == 0
    return pl.pallas_call(
        functools.partial(_proj_kernel, body=body, n_in=len(ins), n_prev=len(aliases),
                          n_out=len(outs), cast_w=first,
                          act_from=None if act_col is None else act_col // tn, side=side_key),
        grid=(ni, nj),
        in_specs=in_specs,
        out_specs=out_specs,
        out_shape=out_shape,
        input_output_aliases=aliases,
        scratch_shapes=list(scratch),
        compiler_params=_params(("arbitrary", "arbitrary")),
        name=name + {"first": "_first", "rest": "_rest", "all": ""}[mode],
    )(*args)


def _gelu_tanh(x):
    c = math.sqrt(2.0 / math.pi)
    inner = x * (c + (c * 0.044715) * (x * x))
    return (0.5 * x) * (1.0 + jnp.tanh(inner))


def _scale_body(acc, ins, outs, scratch, act=False):
    (inv_ref,), (o_ref,) = ins, outs
    t = acc * inv_ref[...]
    o_ref[...] = (_gelu_tanh(t) if act else t).astype(o_ref.dtype)


def _relu2_body(acc, ins, outs, scratch):
    (inv_ref,), (o_ref,) = ins, outs
    r = jnp.maximum(acc * inv_ref[...], 0.0)
    o_ref[...] = (r * r).astype(o_ref.dtype)


def _residual_norm_body(acc, ins, outs, scratch, *, d_model):
    (x_ref, g_ref), (x1_ref, x1g_ref, inv_ref), (ss_ref,) = ins, outs, scratch
    j = pl.program_id(1)
    x1 = x_ref[...] + acc
    x1_ref[...] = x1
    x1g_ref[...] = (x1 * g_ref[...]).astype(BF16)
    part = jnp.sum(x1 * x1, axis=-1, keepdims=True)

    @pl.when(j == 0)
    def _():
        ss_ref[...] = part

    @pl.when(j > 0)
    def _():
        ss_ref[...] += part

    @pl.when(j == pl.num_programs(1) - 1)
    def _():
        inv_ref[...] = lax.rsqrt(ss_ref[...] * (1.0 / d_model) + EPS)


PREP_ROWS = 256


def _inproj_rest_kernel(x_ref, g_ref, w_ref, z_prev_ref, o_ref, xg_scr, inv_scr,
                        *, n_row_blocks, prep_steps, act_from):
    del z_prev_ref
    r = pl.program_id(0)
    j = pl.program_id(1)
    slot_in, slot_out = lax.rem(r, 2), lax.rem(r + 1, 2)

    def project(act):
        acc = jnp.dot(xg_scr[slot_in], w_ref[...], preferred_element_type=F32)
        t = acc * inv_scr[slot_in]
        o_ref[...] = (_gelu_tanh(t) if act else t).astype(o_ref.dtype)

    def prep():
        x = x_ref[...]
        rows = pl.ds(pl.multiple_of(j * PREP_ROWS, PREP_ROWS), PREP_ROWS)
        ss = jnp.sum(x * x, axis=-1, keepdims=True)
        inv_scr[slot_out, rows, :] = lax.rsqrt(ss * (1.0 / x.shape[-1]) + EPS)
        xg_scr[slot_out, rows, :] = (x * g_ref[...]).astype(BF16)

    has_next = r < n_row_blocks - 1
    prepping = has_next & (j < prep_steps)

    @pl.when((r == 0) & prepping)
    def _():
        prep()

    for act in (False, True):
        variant = (r > 0) & ((j >= act_from) == act)

        @pl.when(variant & prepping)
        def _():
            project(act)
            prep()

        @pl.when(variant & jnp.logical_not(prepping))
        def _():
            project(act)


def _inproj(x2d, gain, w_in, act_col, tm=1024, tn=1024):
    m, k = x2d.shape
    n = w_in.shape[1]
    ni, nj = m // tm, n // tn
    prep_steps = tm // PREP_ROWS
    assert prep_steps <= nj and act_col % tn == 0
    xg0, inv0 = _prep(x2d, gain, tm)
    z_part, wb = _project(_scale_body, "in_proj", xg0, w_in, [(inv0, "row")],
                          [(jax.ShapeDtypeStruct((m, n), BF16), "tile")], tm=tm, tn=FIRST_TN,
                          mode="first", act_col=act_col)

    def x_map(r, j):
        return (jnp.minimum(r + 1, ni - 1) * prep_steps + jnp.minimum(j, prep_steps - 1), 0)

    def col(r, j):
        return jnp.where(r == 0, 0, j)

    return pl.pallas_call(
        functools.partial(_inproj_rest_kernel, n_row_blocks=ni, prep_steps=prep_steps,
                          act_from=act_col // tn),
        grid=(ni, nj),
        in_specs=[pl.BlockSpec((PREP_ROWS, k), x_map),
                  pl.BlockSpec((1, k), lambda r, j: (0, 0)),
                  pl.BlockSpec((k, tn), lambda r, j: (0, col(r, j))),
                  pl.BlockSpec(memory_space=pl.ANY)],
        out_specs=pl.BlockSpec((tm, tn), lambda r, j: (jnp.maximum(r, 1), col(r, j))),
        out_shape=jax.ShapeDtypeStruct((m, n), BF16),
        input_output_aliases={3: 0},
        scratch_shapes=[pltpu.VMEM((2, tm, k), BF16), pltpu.VMEM((2, tm, 1), F32)],
        compiler_params=_params(("arbitrary", "arbitrary")),
        name="in_proj_rest",
    )(x2d, gain.reshape(1, k), wb, z_part)


def _up(x1g, w1, inv, w_next, tm=1024, tn=1024):
    m, n = x1g.shape[0], w1.shape[1]
    ins, outs = [(inv, "row")], [(jax.ShapeDtypeStruct((m, n), BF16), "tile")]
    hid_part, w1b = _project(_relu2_body, "mlp_up", x1g, w1, ins, outs, tm=tm, tn=FIRST_TN,
                             mode="first")
    return _project(_relu2_body, "mlp_up", x1g, w1b, ins, outs, tm=tm, tn=tn, mode="rest",
                    prev=[hid_part], side=(w_next, None))


def _outproj(mix, w_outb, x2d, gain, tm=1024, tn=512):
    m, n = x2d.shape
    outs = [(jax.ShapeDtypeStruct((m, n), F32), "tile"),
            (jax.ShapeDtypeStruct((m, n), BF16), "tile"),
            (jax.ShapeDtypeStruct((m, 1), F32), "row")]
    return _project(functools.partial(_residual_norm_body, d_model=n), "out_proj", mix, w_outb,
                    [(x2d, "tile"), (gain.reshape(1, n), "col")], outs, tm=tm, tn=tn, mode="all",
                    scratch=[pltpu.VMEM((tm, 1), F32)])


def _down_kernel(a_ref, w_ref, x1_ref, o_ref):
    @pl.when(pl.program_id(2) == 0)
    def _():
        o_ref[...] = x1_ref[...]

    o_ref[...] += jnp.dot(a_ref[...], w_ref[...], preferred_element_type=F32)


def _down(hid, w2b, x1, tm=1024, tn=1024, tk=4096):
    m, k = hid.shape
    n = w2b.shape[1]
    return pl.pallas_call(
        _down_kernel,
        grid=(m // tm, n // tn, k // tk),
        in_specs=[pl.BlockSpec((tm, tk), lambda i, j, kk: (i, kk)),
                  pl.BlockSpec((tk, tn), lambda i, j, kk: (kk, j)),
                  pl.BlockSpec((tm, tn), lambda i, j, kk: (i, j))],
        out_specs=pl.BlockSpec((tm, tn), lambda i, j, kk: (i, j)),
        out_shape=jax.ShapeDtypeStruct((m, n), F32),
        compiler_params=_params(("arbitrary", "arbitrary", "arbitrary")),
        name="mlp_down",
    )(hid, w2b, x1)


def _t5_bucket(rel):
    nb = N_BUCKETS // 2
    max_exact = nb // 2
    ret = (rel > 0).astype(np.int32) * nb
    n = np.abs(rel)
    large = max_exact + (np.log(np.maximum(n, 1).astype(np.float32) / max_exact)
                         / math.log(MAX_DISTANCE / max_exact) * (nb - max_exact)).astype(np.int32)
    large = np.minimum(large, nb - 1)
    return ret + np.where(n < max_exact, n, large)


def _bucket_table():
    a = np.arange(BLOCK)[:, None]
    s = np.arange(3 * BLOCK)[None, :]
    rel = s - BLOCK - a
    return np.where(np.abs(rel) <= WINDOW, _t5_bucket(rel), -1).astype(np.int32)


def _fill_bias_tables(rb_ref, bk_ref, bias_scr):
    bk = bk_ref[...]
    col = lax.broadcasted_iota(jnp.int32, bk.shape, 1)
    for h in range(N_Q_HEADS):
        acc = jnp.full(bk.shape, NEG, F32)
        for b in range(N_BUCKETS):
            acc = jnp.where(bk == b, rb_ref[b, h] * LOG2E, acc)
        bias_scr[0, h] = jnp.where(col >= BLOCK, acc, NEG)
        bias_scr[1, h] = acc
        bias_scr[2, h] = jnp.where(col < 2 * BLOCK, acc, NEG)


def _rms_scale(t):
    return lax.rsqrt(jnp.sum(t * t, axis=-1, keepdims=True) * (1.0 / t.shape[-1]) + EPS)


MIX_BLOCKS = 2
Z_COL_BLOCK = 1024


def _mix_kernel(sink_ref, q_ref, kp_ref, ko_ref, kn_ref, vp_ref, vo_ref, vn_ref,
                u0_ref, u1_ref, g0_ref, g1_ref, rb_ref, bk_ref, qg_ref, kg_ref,
                vg_ref, ws_ref, bt_ref, side_src, side_scale, mix_ref, side_dst, a_scr, g_scr,
                bias_scr, *, n_chunks):
    n = pl.program_id(1)
    nlast = pl.num_programs(1) - 1

    @pl.when((pl.program_id(0) == 0) & (n == 0))
    def _():
        _fill_bias_tables(rb_ref, bk_ref, bias_scr)

    kfold = kg_ref[...] * qg_ref[...] * (HEAD_DIM ** -0.5 * LOG2E)

    kband = jnp.concatenate([kp_ref[...], ko_ref[...], kn_ref[...]], axis=0).astype(F32)
    vband = jnp.concatenate([vp_ref[...], vo_ref[...], vn_ref[...]], axis=0)
    ones = jnp.ones((3 * BLOCK, HEAD_DIM), BF16)
    for kh in range(N_KV_HEADS):
        k = kband[:, kh * HEAD_DIM:(kh + 1) * HEAD_DIM]
        kn_all = (k * _rms_scale(k) * kfold).astype(BF16)
        for sb in range(MIX_BLOCKS):
            tok = slice(sb * BLOCK, (sb + 1) * BLOCK)
            band = slice(sb * BLOCK, (sb + 3) * BLOCK)
            if sb == 0:
                variant = jnp.where(n == 0, 0, 1)
            elif sb == MIX_BLOCKS - 1:
                variant = jnp.where(n == nlast, 2, 1)
            else:
                variant = 1
            kn = kn_all[band]
            v1 = jnp.concatenate([vband[band, kh * HEAD_DIM:(kh + 1) * HEAD_DIM], ones], axis=1)
            qs = []
            for g in range(GQA_GROUP):
                h = kh * GQA_GROUP + g
                qh = q_ref[tok, h * HEAD_DIM:(h + 1) * HEAD_DIM].astype(F32)
                qs.append((qh * _rms_scale(qh)).astype(BF16))
            qstack = jnp.concatenate(qs, axis=0)
            s = lax.dot_general(qstack, kn, (((1,), (1,)), ((), ())),
                                preferred_element_type=F32)
            ps, sink_terms = [], []
            for g in range(GQA_GROUP):
                h = kh * GQA_GROUP + g
                sink = sink_ref[h] * LOG2E
                sg = s[g * BLOCK:(g + 1) * BLOCK] + bias_scr[variant, h]
                m = jnp.maximum(jnp.max(sg, axis=-1, keepdims=True), sink)
                ps.append(jnp.exp2(sg - m).astype(BF16))
                sink_terms.append(jnp.exp2(sink - m))
            pstack = jnp.concatenate(ps, axis=0)
            ol = jnp.dot(pstack, v1, preferred_element_type=F32)
            for g in range(GQA_GROUP):
                h = kh * GQA_GROUP + g
                rows = slice(g * BLOCK, (g + 1) * BLOCK)
                denom = ol[rows, HEAD_DIM:] + sink_terms[g]
                a_scr[tok, h * HEAD_DIM:(h + 1) * HEAD_DIM] = ol[rows, :HEAD_DIM] * (1.0 / denom)

    a = a_scr[...]
    mix_ref[:, :ATTN_W] = (a * _rms_scale(a)).astype(mix_ref.dtype)

    gu = jnp.concatenate([u0_ref[...], u1_ref[...]], axis=1).astype(F32)
    gv = jnp.concatenate([g0_ref[...], g1_ref[...]], axis=1).astype(F32)
    vn_ = (gv * _rms_scale(gv) * vg_ref[...]).astype(BF16)
    for sb in range(MIX_BLOCKS):
        tok = slice(sb * BLOCK, (sb + 1) * BLOCK)
        for h in range(GMLP_HEADS):
            sl = slice(h * HEAD_DIM, (h + 1) * HEAD_DIM)
            sv = jnp.dot(ws_ref[h], vn_[tok, sl], preferred_element_type=F32) + bt_ref[:, h:h + 1]
            g_scr[tok, sl] = gu[tok, sl] * sv
    gg = g_scr[...]
    mix_ref[:, ATTN_W:] = (gg * _rms_scale(gg)).astype(mix_ref.dtype)
    _cast_chunk(pl.program_id(0) * pl.num_programs(1) + n, n_chunks, side_src, side_scale, side_dst)


def _mix(z3, rel_bias, sink, q_gain, k_gain, v_gain, w_s, b_t, w_next, scale_next):
    bsz, seq, _ = z3.shape
    nblk = seq // BLOCK
    rows = MIX_BLOCKS * BLOCK
    assert nblk >= 2 and nblk % MIX_BLOCKS == 0
    kcol = ATTN_W // KV_W
    vcol = kcol + 1
    ucol = (ATTN_W + 2 * KV_W) // Z_COL_BLOCK
    gcol = ucol + GMLP_W // Z_COL_BLOCK
    assert (ATTN_W + 2 * KV_W) % Z_COL_BLOCK == 0 and GMLP_W == 2 * Z_COL_BLOCK
    n_steps = nblk // MIX_BLOCKS
    n_chunks, s_in, s_args, s_out_spec, s_out_shape = _side_specs(
        w_next, scale_next, bsz * n_steps, lambda b, n: b * n_steps + n)

    def own(width, col):
        return pl.BlockSpec((None, rows, width), lambda b, n: (b, n, col))

    def edge(width, col, after):
        def imap(b, n):
            blk = n * MIX_BLOCKS + (MIX_BLOCKS if after else -1)
            return (b, jnp.clip(blk, 0, nblk - 1), col)
        return pl.BlockSpec((None, BLOCK, width), imap)

    def full(shape):
        return pl.BlockSpec(shape, lambda b, n: (0,) * len(shape))

    in_specs = [
        pl.BlockSpec(memory_space=pltpu.SMEM),
        own(ATTN_W, 0),
        edge(KV_W, kcol, False), own(KV_W, kcol), edge(KV_W, kcol, True),
        edge(KV_W, vcol, False), own(KV_W, vcol), edge(KV_W, vcol, True),
        own(Z_COL_BLOCK, ucol), own(Z_COL_BLOCK, ucol + 1),
        own(Z_COL_BLOCK, gcol), own(Z_COL_BLOCK, gcol + 1),
        pl.BlockSpec(memory_space=pltpu.SMEM),
        full((BLOCK, 3 * BLOCK)),
        full((1, HEAD_DIM)), full((1, HEAD_DIM)),
        full((1, GMLP_W)), full(w_s.shape), full(b_t.shape),
    ] + s_in
    return pl.pallas_call(
        functools.partial(_mix_kernel, n_chunks=n_chunks),
        grid=(bsz, n_steps),
        in_specs=in_specs,
        out_specs=[pl.BlockSpec((None, rows, ATTN_W + GMLP_W), lambda b, n: (b, n, 0)), s_out_spec],
        out_shape=[jax.ShapeDtypeStruct((bsz, seq, ATTN_W + GMLP_W), BF16), s_out_shape],
        scratch_shapes=[pltpu.VMEM((rows, ATTN_W), F32), pltpu.VMEM((rows, GMLP_W), F32),
                        pltpu.VMEM((3, N_Q_HEADS, BLOCK, 3 * BLOCK), F32)],
        compiler_params=_params(("arbitrary", "arbitrary")),
        name="mix",
    )(sink, z3, z3, z3, z3, z3, z3, z3, z3, z3, z3, z3, rel_bias, jnp.asarray(_bucket_table()),
      q_gain.reshape(1, HEAD_DIM), k_gain.reshape(1, HEAD_DIM),
      v_gain.reshape(1, GMLP_W), w_s, b_t, *s_args)


def kernel(x, norm1, w_in, q_gain, k_gain, rel_bias, attn_sink, attn_out_gain,
           gmlp_v_gain, gmlp_w_s, gmlp_b_s, gmlp_out_gain, w_out, norm2, w1, w2):
    bsz, seq, d = x.shape
    depth = norm1.shape[0]
    x2d = x.reshape(bsz * seq, d)
    for l in range(depth):
        mix_gain = jnp.concatenate([attn_out_gain[l], gmlp_out_gain[l]])
        z = _inproj(x2d, norm1[l], w_in[l], ATTN_W + 2 * KV_W)
        mix, w_outb = _mix(z.reshape(bsz, seq, -1), rel_bias, attn_sink[l], q_gain[l], k_gain[l],
                           gmlp_v_gain[l], gmlp_w_s[l].astype(BF16), gmlp_b_s[l].T, w_out[l], mix_gain)
        x1, x1g, inv2 = _outproj(mix.reshape(bsz * seq, -1), w_outb, x2d, norm2[l])
        hid, w2b = _up(x1g, w1[l], inv2, w2[l])
        x2d = _down(hid, w2b, x1)
    return x2d.reshape(bsz, seq, d)
```
